```python
import math
import jax
import jax.numpy as jnp
from jax import lax
import numpy as np

D_MODEL = 1024
BATCH = 8
SEQ = 4096
DEPTH = 1
DEC_BATCH = 128
DEC_SEQ = 1
PAST_LEN = 8192
PAGE_SIZE = 128

DIFF_HEADS = 4
DIFF_HEAD_DIM = 64
DIFF_V_DIM = 2 * DIFF_HEAD_DIM
DIFF_QK = DIFF_HEADS * 2 * DIFF_HEAD_DIM
DIFF_WIDTH = DIFF_HEADS * DIFF_V_DIM
DIFF_SCALE = DIFF_HEAD_DIM ** -0.5
FOX_HEADS = 8
FOX_HEAD_DIM = 64
FOX_WIDTH = FOX_HEADS * FOX_HEAD_DIM
FOX_SCALE = FOX_HEAD_DIM ** -0.5
REL_BUCKETS = 32
REL_MAX_DIST = 128
PEER_HEADS = 8
PEER_N_KEYS = 128
PEER_N_EXPERTS = PEER_N_KEYS * PEER_N_KEYS
PEER_QUERY_DIM = 256
PEER_HALF = PEER_QUERY_DIM // 2
PEER_TOPK = 16
PEER_BLOCK = 128
PLE_DIM = 256
Q_BLOCK = 128
EPS = 1e-6
NEG_INF = -1e30
IN_SPLITS = (DIFF_QK, DIFF_QK, DIFF_WIDTH, FOX_WIDTH, FOX_WIDTH, FOX_WIDTH, FOX_HEADS, D_MODEL, D_MODEL)
IN_COLS = sum(IN_SPLITS)

kernel_name = 'diff_fox_peer_hybrid_step'


def rmsnorm(x, g):
    xf = x.astype(jnp.float32)
    y = xf * lax.rsqrt(jnp.mean(xf * xf, axis=-1, keepdims=True) + EPS)
    return (y * g.astype(jnp.float32)).astype(x.dtype)


def t5_bucket(rel):
    n = jnp.maximum(rel, 0)
    max_exact = REL_BUCKETS // 2
    nf = jnp.maximum(n, 1).astype(jnp.float32)
    large = max_exact + (jnp.log(nf / max_exact) / math.log(REL_MAX_DIST / max_exact)
                         * (REL_BUCKETS - max_exact)).astype(jnp.int32)
    large = jnp.minimum(large, REL_BUCKETS - 1)
    return jnp.where(n < max_exact, n, large)


def rel_bias(rel, rel_table):
    return jnp.moveaxis(rel_table[t5_bucket(rel)], -1, 0).astype(jnp.float32)


def project_in(h, w_in, b_f):
    b, s = h.shape[:2]
    z = jnp.einsum('bsd,dc->bsc', h, w_in)
    cuts = [int(c) for c in np.cumsum(IN_SPLITS)[:-1]]
    dq, dk, dv, fq, fk, fv, fl, ga, gb = jnp.split(z, cuts, axis=-1)
    return (dq.reshape(b, s, DIFF_HEADS, 2, DIFF_HEAD_DIM),
            dk.reshape(b, s, DIFF_HEADS, 2, DIFF_HEAD_DIM),
            dv.reshape(b, s, DIFF_HEADS, DIFF_V_DIM),
            fq.reshape(b, s, FOX_HEADS, FOX_HEAD_DIM),
            fk.reshape(b, s, FOX_HEADS, FOX_HEAD_DIM),
            fv.reshape(b, s, FOX_HEADS, FOX_HEAD_DIM),
            jax.nn.log_sigmoid((fl + b_f).astype(jnp.float32)),
            jax.nn.sigmoid(ga), jax.nn.sigmoid(gb))


def diff_attend(q, k, v, bias, mask, lam, g_subln, lam_init):
    s = jnp.einsum('bqhmd,bkhmd->bmhqk', q, k).astype(jnp.float32) * DIFF_SCALE + bias
    s = jnp.where(mask, s, NEG_INF)
    p = jax.nn.softmax(s, axis=-1)
    a = p[:, 0] - lam * p[:, 1]
    o = jnp.einsum('bhqk,bkhe->bqhe', a.astype(v.dtype), v)
    return rmsnorm(o, g_subln) * (1.0 - lam_init)


def fox_attend(q, k, v, cq, ck, mask):
    s = jnp.einsum('bqhd,bkhd->bhqk', q, k).astype(jnp.float32) * FOX_SCALE
    s = s + jnp.swapaxes(cq, 1, 2)[..., :, None] - jnp.swapaxes(ck, 1, 2)[..., None, :]
    s = jnp.where(mask, s, NEG_INF)
    p = jax.nn.softmax(s, axis=-1)
    return jnp.einsum('bhqk,bkhd->bqhd', p.astype(v.dtype), v)


def prompt_attention(dq, dk, dv, fq, fk, fv, logf, rel_table, lam, g_subln, lam_init):
    b, s = dq.shape[:2]
    c = jnp.cumsum(logf, axis=1)
    kpos = jnp.arange(s)

    def block(i):
        start = i * Q_BLOCK
        qpos = start + jnp.arange(Q_BLOCK)
        rel = qpos[:, None] - kpos[None, :]
        mask = rel >= 0
        bias = rel_bias(rel, rel_table)
        sl = lambda a: lax.dynamic_slice_in_dim(a, start, Q_BLOCK, axis=1)
        od = diff_attend(sl(dq), dk, dv, bias, mask, lam, g_subln, lam_init)
        of = fox_attend(sl(fq), fk, fv, sl(c), c, mask)
        return od, of

    od, of = lax.map(block, jnp.arange(s // Q_BLOCK))
    od = jnp.moveaxis(od, 0, 1).reshape(b, s, DIFF_WIDTH)
    of = jnp.moveaxis(of, 0, 1).reshape(b, s, FOX_WIDTH)
    return od, of


def sample_attention(dq, dk, dv, fq, fk, fv, logf, cache_diff_k, cache_diff_v, cache_fox_k,
                     cache_fox_v, cache_fox_logf, layer, page_table, rel_table, lam, g_subln, lam_init):
    nb, ns = dq.shape[:2]
    past = page_table.shape[1] * cache_diff_k.shape[2]
    qpos = past + jnp.arange(ns)
    kpos = jnp.arange(past + ns)
    rel = qpos[:, None] - kpos[None, :]
    mask = rel >= 0
    bias = rel_bias(rel, rel_table)

    def gather(cache, pt, like):
        rows = cache[layer, pt]
        return rows.reshape((past,) + cache.shape[3:]).astype(like.dtype)

    def one(args):
        pt, q_d, k_d, v_d, q_f, k_f, v_f, lf = args
        kd = jnp.concatenate([gather(cache_diff_k, pt, k_d), k_d], axis=0)
        vd = jnp.concatenate([gather(cache_diff_v, pt, v_d), v_d], axis=0)
        kf = jnp.concatenate([gather(cache_fox_k, pt, k_f), k_f], axis=0)
        vf = jnp.concatenate([gather(cache_fox_v, pt, v_f), v_f], axis=0)
        lf_all = jnp.concatenate([gather(cache_fox_logf, pt, lf), lf], axis=0).astype(jnp.float32)
        c = jnp.cumsum(lf_all, axis=0)
        od = diff_attend(q_d[None], kd[None], vd[None], bias, mask, lam, g_subln, lam_init)[0]
        of = fox_attend(q_f[None], kf[None], vf[None], c[past:][None], c[None], mask)[0]
        return od, of

    od, of = lax.map(one, (page_table, dq, dk, dv, fq, fk, fv, logf))
    return od.reshape(nb, ns, DIFF_WIDTH), of.reshape(nb, ns, FOX_WIDTH)


def merge_branches(x, od, of, ga, gb, w_pa, w_pb, w_o):
    ya = jnp.einsum('bse,ed->bsd', od, w_pa)
    yb = jnp.einsum('bse,ed->bsd', of, w_pb)
    return x + jnp.einsum('bsd,de->bse', ga * ya + gb * yb, w_o)


def peer_ffn(h, w_pq, sub_k1, sub_k2, peer_u, peer_v):
    shp = h.shape
    t = h.reshape(-1, shp[-1])
    n = t.shape[0]
    nblk = -(-n // PEER_BLOCK)
    t = jnp.pad(t, ((0, nblk * PEER_BLOCK - n), (0, 0)))

    def blk(hb):
        q = jnp.einsum('td,dc->tc', hb, w_pq).reshape(PEER_BLOCK, PEER_HEADS, 2, PEER_HALF)
        s1 = jnp.einsum('thc,nc->thn', q[:, :, 0], sub_k1).astype(jnp.float32)
        s2 = jnp.einsum('thc,nc->thn', q[:, :, 1], sub_k2).astype(jnp.float32)
        a, ia = lax.top_k(s1, PEER_TOPK)
        bsc, ib = lax.top_k(s2, PEER_TOPK)
        cand = (a[..., :, None] + bsc[..., None, :]).reshape(PEER_BLOCK, PEER_HEADS, PEER_TOPK * PEER_TOPK)
        sc, ci = lax.top_k(cand, PEER_TOPK)
        e = (jnp.take_along_axis(ia, ci // PEER_TOPK, axis=-1) * PEER_N_KEYS
             + jnp.take_along_axis(ib, ci % PEER_TOPK, axis=-1))
        g = jax.nn.softmax(sc, axis=-1)
        act = jax.nn.gelu(jnp.einsum('td,thkd->thk', hb, peer_u[e]).astype(jnp.float32))
        return jnp.einsum('thk,thkd->td', (g * act).astype(hb.dtype), peer_v[e])

    out = lax.map(blk, t.reshape(nblk, PEER_BLOCK, shp[-1]))
    return out.reshape(-1, shp[-1])[:n].reshape(shp)


def channel_and_ple(x, p, g_ffn, w_pq, sub_k1, sub_k2, peer_u, peer_v, g_ple, w_ple_gate, w_ple_proj):
    x = x + peer_ffn(rmsnorm(x, g_ffn), w_pq, sub_k1, sub_k2, peer_u, peer_v)
    gate = jax.nn.sigmoid(jnp.einsum('bsd,de->bse', rmsnorm(x, g_ple), w_ple_gate))
    return x + gate * jnp.einsum('bsp,pd->bsd', p, w_ple_proj)


def setup_inputs(seed: int = 0) -> dict:
    key = jax.random.key(seed)
    ks = list(jax.random.split(key, 40))
    nrm = lambda shape, scale: jax.random.normal(ks.pop(), shape, jnp.float32) * scale
    n_pages = PAST_LEN // PAGE_SIZE
    n_pool = (DEC_BATCH * n_pages * 5) // 4
    page_table = jax.random.permutation(ks.pop(), n_pool)[: DEC_BATCH * n_pages]
    page_table = page_table.reshape(DEC_BATCH, n_pages).astype(jnp.int32)
    return {
        'x_prompt': nrm((BATCH, SEQ, D_MODEL), 1.0),
        'x_sample': nrm((DEC_BATCH, DEC_SEQ, D_MODEL), 1.0),
        'p_prompt': nrm((DEPTH, BATCH, SEQ, PLE_DIM), 1.0),
        'p_sample': nrm((DEPTH, DEC_BATCH, DEC_SEQ, PLE_DIM), 1.0),
        'cache_diff_k': nrm((DEPTH, n_pool, PAGE_SIZE, DIFF_HEADS, 2, DIFF_HEAD_DIM), 1.0),
        'cache_diff_v': nrm((DEPTH, n_pool, PAGE_SIZE, DIFF_HEADS, DIFF_V_DIM), 1.0),
        'cache_fox_k': nrm((DEPTH, n_pool, PAGE_SIZE, FOX_HEADS, FOX_HEAD_DIM), 1.0),
        'cache_fox_v': nrm((DEPTH, n_pool, PAGE_SIZE, FOX_HEADS, FOX_HEAD_DIM), 1.0),
        'cache_fox_logf': jax.nn.log_sigmoid(2.0 + nrm((DEPTH, n_pool, PAGE_SIZE, FOX_HEADS), 1.0)),
        'page_table': page_table,
        'rel_table': nrm((REL_BUCKETS, DIFF_HEADS), 0.5),
        'g_attn': 1.0 + nrm((DEPTH, D_MODEL), 0.1),
        'w_in': nrm((DEPTH, D_MODEL, IN_COLS), D_MODEL ** -0.5),
        'b_f': 2.0 + nrm((DEPTH, FOX_HEADS), 0.1),
        'lam_q1': nrm((DEPTH, DIFF_HEAD_DIM), 0.1),
        'lam_k1': nrm((DEPTH, DIFF_HEAD_DIM), 0.1),
        'lam_q2': nrm((DEPTH, DIFF_HEAD_DIM), 0.1),
        'lam_k2': nrm((DEPTH, DIFF_HEAD_DIM), 0.1),
        'g_subln': 1.0 + nrm((DEPTH, DIFF_V_DIM), 0.1),
        'w_pa': nrm((DEPTH, DIFF_WIDTH, D_MODEL), DIFF_WIDTH ** -0.5),
        'w_pb': nrm((DEPTH, FOX_WIDTH, D_MODEL), FOX_WIDTH ** -0.5),
        'w_o': nrm((DEPTH, D_MODEL, D_MODEL), D_MODEL ** -0.5),
        'g_ffn': 1.0 + nrm((DEPTH, D_MODEL), 0.1),
        'w_pq': nrm((DEPTH, D_MODEL, PEER_HEADS * PEER_QUERY_DIM), D_MODEL ** -0.5),
        'sub_k1': nrm((DEPTH, PEER_N_KEYS, PEER_HALF), PEER_HALF ** -0.5),
        'sub_k2': nrm((DEPTH, PEER_N_KEYS, PEER_HALF), PEER_HALF ** -0.5),
        'peer_u': nrm((DEPTH, PEER_N_EXPERTS, D_MODEL), D_MODEL ** -0.5),
        'peer_v': nrm((DEPTH, PEER_N_EXPERTS, D_MODEL), PEER_HEADS ** -0.5),
        'g_ple': 1.0 + nrm((DEPTH, D_MODEL), 0.1),
        'w_ple_gate': nrm((DEPTH, D_MODEL, D_MODEL), D_MODEL ** -0.5),
        'w_ple_proj': nrm((DEPTH, PLE_DIM, D_MODEL), PLE_DIM ** -0.5),
        'g_final': 1.0 + nrm((D_MODEL,), 0.1),
    }


def reference(x_prompt, x_sample, p_prompt, p_sample, cache_diff_k, cache_diff_v, cache_fox_k,
              cache_fox_v, cache_fox_logf, page_table, rel_table, g_attn, w_in, b_f, lam_q1, lam_k1,
              lam_q2, lam_k2, g_subln, w_pa, w_pb, w_o, g_ffn, w_pq, sub_k1, sub_k2, peer_u, peer_v,
              g_ple, w_ple_gate, w_ple_proj, g_final):
    xp, xs = x_prompt, x_sample
    dk_p, dv_p, fk_p, fv_p, lf_p = [], [], [], [], []
    dk_s, dv_s, fk_s, fv_s, lf_s = [], [], [], [], []
    for l in range(DEPTH):
        lam_init = 0.8 - 0.6 * math.exp(-0.3 * l)
        lam = (jnp.exp(jnp.sum((lam_q1[l] * lam_k1[l]).astype(jnp.float32)))
               - jnp.exp(jnp.sum((lam_q2[l] * lam_k2[l]).astype(jnp.float32))) + lam_init)
        dq, dk, dv, fq, fk, fv, lf, ga, gb = project_in(rmsnorm(xp, g_attn[l]), w_in[l], b_f[l])
        od, of = prompt_attention(dq, dk, dv, fq, fk, fv, lf, rel_table, lam, g_subln[l], lam_init)
        xp = merge_branches(xp, od, of, ga, gb, w_pa[l], w_pb[l], w_o[l])
        xp = channel_and_ple(xp, p_prompt[l], g_ffn[l], w_pq[l], sub_k1[l], sub_k2[l], peer_u[l],
                             peer_v[l], g_ple[l], w_ple_gate[l], w_ple_proj[l])
        dk_p.append(dk); dv_p.append(dv); fk_p.append(fk); fv_p.append(fv); lf_p.append(lf)
        dq, dk, dv, fq, fk, fv, lf, ga, gb = project_in(rmsnorm(xs, g_attn[l]), w_in[l], b_f[l])
        od, of = sample_attention(dq, dk, dv, fq, fk, fv, lf, cache_diff_k, cache_diff_v, cache_fox_k,
                                  cache_fox_v, cache_fox_logf, l, page_table, rel_table, lam,
                                  g_subln[l], lam_init)
        xs = merge_branches(xs, od, of, ga, gb, w_pa[l], w_pb[l], w_o[l])
        xs = channel_and_ple(xs, p_sample[l], g_ffn[l], w_pq[l], sub_k1[l], sub_k2[l], peer_u[l],
                             peer_v[l], g_ple[l], w_ple_gate[l], w_ple_proj[l])
        dk_s.append(dk); dv_s.append(dv); fk_s.append(fk); fv_s.append(fv); lf_s.append(lf)
    y_prompt = rmsnorm(xp, g_final)
    y_sample = rmsnorm(xs, g_final)
    return (y_prompt, y_sample,
            jnp.stack(dk_p), jnp.stack(dv_p), jnp.stack(fk_p), jnp.stack(fv_p), jnp.stack(lf_p),
            jnp.stack(dk_s), jnp.stack(dv_s), jnp.stack(fk_s), jnp.stack(fv_s), jnp.stack(lf_s))
```

```python
import functools
import math

import jax
import jax.numpy as jnp
from jax import lax
from jax.experimental import pallas as pl
from jax.experimental.pallas import tpu as pltpu

F32 = jnp.float32
BF16 = jnp.bfloat16
U32 = jnp.uint32
I32 = jnp.int32

D_MODEL = 1024
DIFF_HEADS = 4
DIFF_HEAD_DIM = 64
DIFF_V_DIM = 128
DIFF_WIDTH = 512
FOX_HEADS = 8
FOX_HEAD_DIM = 64
FOX_WIDTH = 512
ATTN_SCALE = 0.125
REL_BUCKETS = 32
REL_MAX_DIST = 128
PEER_HEADS = 8
PEER_N_KEYS = 128
PEER_HALF = 128
PEER_TOPK = 16
PEER_SEL = PEER_HEADS * PEER_TOPK
PLE_DIM = 256
EPS = 1e-6
NEG_INF = -1e30
LANES = 128
HALF_D = D_MODEL // 2
VMEM_LIMIT = 56 * 1024 * 1024

NT_DIMS = (((1,), (1,)), ((), ()))


def _const_spec(shape):
    nd = len(shape)
    return pl.BlockSpec(shape, lambda *_: (0,) * nd, pipeline_mode=pl.Buffered(1))


def _params(sem, vmem=VMEM_LIMIT):
    return pltpu.CompilerParams(dimension_semantics=sem, vmem_limit_bytes=vmem)


def _rms(x, g):
    return x * lax.rsqrt(jnp.mean(x * x, axis=-1, keepdims=True) + EPS) * g


def _log_sigmoid(x):
    return -(jnp.maximum(-x, 0.0) + jnp.log1p(jnp.exp(-jnp.abs(x))))


def _unpack_lo(w):
    return lax.bitcast_convert_type(w << 16, F32)


def _unpack_hi(w):
    return lax.bitcast_convert_type(w & jnp.uint32(0xFFFF0000), F32)


def _pack_pairs(lo_f32, hi_f32):
    lo = lax.bitcast_convert_type(lo_f32, U32) >> 16
    hi = lax.bitcast_convert_type(hi_f32, U32) & jnp.uint32(0xFFFF0000)
    return hi | lo


def _inproj_kernel(x_ref, g_ref, wqkv_ref, wfl_ref, wflt_ref, wg_ref, bfr_ref, bfc_ref,
                   dq_ref, dkb_ref, dvb_ref, fq_ref, fkb_ref, fvb_ref,
                   dk_ref, dv_ref, fk_ref, fv_ref, lf_ref, lft_ref, ga_ref, gb_ref):
    h = _rms(x_ref[...], g_ref[...]).astype(BF16)

    def proj(c):
        return jnp.dot(h, wqkv_ref[:, c * 512:(c + 1) * 512], preferred_element_type=F32)

    dq_ref[...] = (proj(0) * ATTN_SCALE).astype(BF16)
    z = proj(1); dk_ref[...] = z; dkb_ref[...] = z.astype(BF16)
    z = proj(2); dv_ref[...] = z; dvb_ref[...] = z.astype(BF16)
    fq_ref[...] = (proj(3) * ATTN_SCALE).astype(BF16)
    z = proj(4); fk_ref[...] = z; fkb_ref[...] = z.astype(BF16)
    z = proj(5); fv_ref[...] = z; fvb_ref[...] = z.astype(BF16)

    fl = jnp.dot(h, wfl_ref[...], preferred_element_type=F32)
    lf_ref[...] = _log_sigmoid(fl[:, :FOX_HEADS] + bfr_ref[...])
    flt = lax.dot_general(wflt_ref[...], h, NT_DIMS, preferred_element_type=F32)
    lft_ref[...] = _log_sigmoid(flt + bfc_ref[...])

    ga_ref[...] = jax.nn.sigmoid(jnp.dot(h, wg_ref[:, :D_MODEL], preferred_element_type=F32))
    gb_ref[...] = jax.nn.sigmoid(jnp.dot(h, wg_ref[:, D_MODEL:], preferred_element_type=F32))


def _inproj(x, g_attn, w_in, b_f, tm):
    n = x.shape[0]
    wqkv = w_in[:, :3072].astype(BF16)
    wfl = jnp.pad(w_in[:, 3072:3080], ((0, 0), (0, LANES - FOX_HEADS))).astype(BF16)
    wflt = w_in[:, 3072:3080].T.astype(BF16)
    wg = w_in[:, 3080:].astype(BF16)
    tok = lambda w: pl.BlockSpec((tm, w), lambda i: (i, 0))
    sds = jax.ShapeDtypeStruct
    out_shape = ([sds((n, 512), BF16)] * 6 + [sds((n, 512), F32)] * 4
                 + [sds((n, FOX_HEADS), F32), sds((FOX_HEADS, n), F32)]
                 + [sds((n, D_MODEL), F32)] * 2)
    out_specs = ([tok(512)] * 10
                 + [tok(FOX_HEADS), pl.BlockSpec((FOX_HEADS, tm), lambda i: (0, i))]
                 + [tok(D_MODEL)] * 2)
    return pl.pallas_call(
        _inproj_kernel,
        grid=(n // tm,),
        in_specs=[tok(D_MODEL), _const_spec((1, D_MODEL)), _const_spec(wqkv.shape),
                  _const_spec(wfl.shape), _const_spec(wflt.shape), _const_spec(wg.shape),
                  _const_spec((1, FOX_HEADS)), _const_spec((FOX_HEADS, 1))],
        out_specs=out_specs,
        out_shape=out_shape,
        compiler_params=_params(("parallel",)),
        name="inproj",
    )(x, g_attn.reshape(1, D_MODEL), wqkv, wfl, wflt, wg,
      b_f.reshape(1, FOX_HEADS), b_f.reshape(FOX_HEADS, 1))


def _lane_prefix(x):
    lane = lax.broadcasted_iota(I32, x.shape, 1)
    sh = 1
    while sh < LANES:
        x = x + jnp.where(lane >= sh, pltpu.roll(x, sh, 1), 0.0)
        sh *= 2
    return x


def _cumsum_kernel(x_ref, o_ref):
    nchunk = x_ref.shape[1] // LANES
    local = [_lane_prefix(x_ref[:, c * LANES:(c + 1) * LANES]) for c in range(nchunk)]
    carry = jnp.zeros((x_ref.shape[0], 1), F32)
    for c in range(nchunk):
        o_ref[:, c * LANES:(c + 1) * LANES] = local[c] + carry
        carry = carry + local[c][:, LANES - 1:LANES]


def _cumsum(lft, seq):
    h, n = lft.shape
    return pl.pallas_call(
        _cumsum_kernel,
        grid=(n // seq,),
        in_specs=[pl.BlockSpec((h, seq), lambda b: (0, b))],
        out_specs=pl.BlockSpec((h, seq), lambda b: (0, b)),
        out_shape=jax.ShapeDtypeStruct((h, n), F32),
        compiler_params=_params(("parallel",)),
        name="cumsum",
    )(lft)


def _t5_bucket(rel):
    n = jnp.maximum(rel, 0)
    max_exact = REL_BUCKETS // 2
    nf = jnp.maximum(n, 1).astype(F32)
    large = max_exact + (jnp.log(nf / max_exact) / math.log(REL_MAX_DIST / max_exact)
                         * (REL_BUCKETS - max_exact)).astype(I32)
    large = jnp.minimum(large, REL_BUCKETS - 1)
    return jnp.where(n < max_exact, n, large)


def _far_bias(rel_table):
    return rel_table[_t5_bucket(jnp.array(REL_MAX_DIST, I32))].astype(F32)


def _shifted_bias(rel, rel_table):
    b = rel_table[_t5_bucket(rel)].astype(F32)
    return jnp.moveaxis(b - _far_bias(rel_table), -1, 0)


def _softmax_step(s, v, m, l, acc):
    m_new = jnp.maximum(m, jnp.max(s, axis=-1, keepdims=True))
    alpha = jnp.exp(m - m_new)
    p = jnp.exp(s - m_new)
    l = alpha * l + jnp.sum(p, axis=-1, keepdims=True)
    acc = alpha * acc + jnp.dot(p.astype(BF16), v, preferred_element_type=F32)
    return m_new, l, acc


def _attn_kernel(diff, t, lam_init, *refs):
    if diff:
        q_ref, k_ref, v_ref, bias_ref, lq1, lk1, lq2, lk2, gs_ref, o_ref = refs
    else:
        q_ref, k_ref, v_ref, c_ref, o_ref = refs
    i = pl.program_id(2)
    lane = lax.broadcasted_iota(I32, (t, LANES), 1)
    q = q_ref[...].astype(F32)
    qa = jnp.where(lane < 64, q, 0.0).astype(BF16)
    qb = jnp.where(lane >= 64, q, 0.0).astype(BF16)

    def step(kb, carry, bias_a, bias_b):
        off = pl.multiple_of(kb * t, t)
        k = k_ref[pl.ds(off, t), :]
        v = v_ref[pl.ds(off, t), :]
        sa = lax.dot_general(qa, k, NT_DIMS, preferred_element_type=F32)
        sb = lax.dot_general(qb, k, NT_DIMS, preferred_element_type=F32)
        if not diff:
            sa = sa - c_ref[0:1, pl.ds(off, t)]
            sb = sb - c_ref[1:2, pl.ds(off, t)]
        if bias_a is not None:
            sa = sa + bias_a
            sb = sb + bias_b
        ma, la, acca, mb, lb, accb = carry
        ma, la, acca = _softmax_step(sa, v, ma, la, acca)
        mb, lb, accb = _softmax_step(sb, v, mb, lb, accb)
        return ma, la, acca, mb, lb, accb

    col0 = jnp.full((t, 1), NEG_INF, F32)
    zero1 = jnp.zeros((t, 1), F32)
    zacc = jnp.zeros((t, LANES), F32)
    carry = (col0, zero1, zacc, col0, zero1, zacc)

    if diff:
        n_plain = jnp.maximum(i - 1, 0)
    else:
        n_plain = i
    carry = lax.fori_loop(0, n_plain, lambda kb, c: step(kb, c, None, None), carry)
    if diff:
        off1 = bias_ref[1]
        carry = lax.cond(i >= 1, lambda c: step(i - 1, c, off1, off1), lambda c: c, carry)
        diag = bias_ref[0]
    else:
        r = lax.broadcasted_iota(I32, (t, t), 0)
        c = lax.broadcasted_iota(I32, (t, t), 1)
        diag = jnp.where(r >= c, 0.0, NEG_INF)
    ma, la, acca, mb, lb, accb = step(i, carry, diag, diag)

    if diff:
        lam = (jnp.exp(jnp.sum(lq1[...] * lk1[...], axis=-1, keepdims=True))
               - jnp.exp(jnp.sum(lq2[...] * lk2[...], axis=-1, keepdims=True)) + lam_init)
        o = acca / la - lam * (accb / lb)
        o_ref[...] = (_rms(o, gs_ref[...]) * (1.0 - lam_init)).astype(o_ref.dtype)
    else:
        o_ref[...] = jnp.where(lane < 64, acca / la, accb / lb).astype(o_ref.dtype)


def _prompt_attn(diff, q, k, v, batch, seq, t, extra, lam_init=0.0):
    nslab = 512 // LANES
    nq = seq // t
    qspec = pl.BlockSpec((t, LANES), lambda b, j, i: (b * nq + i, j))
    kvspec = pl.BlockSpec((seq, LANES), lambda b, j, i: (b, j))
    if diff:
        bias, lq1, lk1, lq2, lk2, gs = extra
        row = _const_spec((1, DIFF_HEAD_DIM))
        especs = [pl.BlockSpec((None, 2, t, t), lambda b, j, i: (j, 0, 0, 0)),
                  row, row, row, row, _const_spec((1, DIFF_V_DIM))]
        eargs = [bias, lq1.reshape(1, -1), lk1.reshape(1, -1), lq2.reshape(1, -1),
                 lk2.reshape(1, -1), gs.reshape(1, -1)]
    else:
        (ct,) = extra
        especs = [pl.BlockSpec((None, 2, seq), lambda b, j, i: (j, 0, b))]
        eargs = [ct.reshape(nslab, 2, batch * seq)]
    return pl.pallas_call(
        functools.partial(_attn_kernel, diff, t, lam_init),
        grid=(batch, nslab, nq),
        in_specs=[qspec, kvspec, kvspec] + especs,
        out_specs=qspec,
        out_shape=jax.ShapeDtypeStruct((batch * seq, 512), BF16),
        compiler_params=_params(("parallel", "parallel", "arbitrary")),
        name="attn_diff" if diff else "attn_fox",
    )(q, k, v, *eargs)


def _paged_kernel(npages, lam_init, pt_ref,
                  qd_ref, kdn_ref, vdn_ref, qf_ref, kfn_ref, vfn_ref, lfn_ref,
                  kd_ref, vd_ref, kf_ref, vf_ref, lft_ref, bias_ref, bself_ref,
                  lq1, lk1, lq2, lk2, gs_ref,
                  od_ref, of_ref,
                  md, ld, accd, mf, lf, accf, csum):
    del pt_ref
    p = pl.program_id(1)
    row = lax.broadcasted_iota(I32, (8, 512), 0)
    lane = lax.broadcasted_iota(I32, (8, 512), 1)
    mask_d = (lane // 64) == row
    mask_f = (lane // 64) == row
    qd = jnp.where(mask_d, qd_ref[0].astype(F32), 0.0)
    qf = jnp.where(mask_f, qf_ref[0].astype(F32), 0.0)

    @pl.when(p == 0)
    def _():
        for m_ref in (md, mf):
            m_ref[...] = jnp.full(m_ref.shape, NEG_INF, F32)
        for z_ref in (ld, lf, accd, accf, csum):
            z_ref[...] = jnp.zeros(z_ref.shape, F32)

    def update(s, v_bf, m_ref, l_ref, acc_ref):
        m_new = jnp.maximum(m_ref[...], jnp.max(s, axis=-1, keepdims=True))
        alpha = jnp.exp(m_ref[...] - m_new)
        pr = jnp.exp(s - m_new)
        l_ref[...] = alpha * l_ref[...] + jnp.sum(pr, axis=-1, keepdims=True)
        acc_ref[...] = alpha * acc_ref[...] + jnp.dot(pr.astype(BF16), v_bf,
                                                      preferred_element_type=F32)
        m_ref[...] = m_new

    sd = lax.dot_general(qd.astype(BF16), kd_ref[...].astype(BF16), NT_DIMS,
                         preferred_element_type=F32)
    sd = sd + jnp.where(p == npages - 1, bias_ref[...], 0.0)
    update(sd, vd_ref[...].astype(BF16), md, ld, accd)

    cpage = _lane_prefix(lft_ref[...]) + csum[...]
    sf = lax.dot_general(qf.astype(BF16), kf_ref[...].astype(BF16), NT_DIMS,
                         preferred_element_type=F32) - cpage
    update(sf, vf_ref[...].astype(BF16), mf, lf, accf)
    csum[...] = cpage[:, LANES - 1:LANES]

    @pl.when(p == npages - 1)
    def _():
        def self_update(s, v_new, m_ref, l_ref, acc_ref):
            m_new = jnp.maximum(m_ref[...], s)
            alpha = jnp.exp(m_ref[...] - m_new)
            pr = jnp.exp(s - m_new)
            l = alpha * l_ref[...] + pr
            return (alpha * acc_ref[...] + pr * v_new) / l

        s_self = jnp.sum(qd * kdn_ref[0].astype(F32), axis=-1, keepdims=True) + bself_ref[...]
        rd = self_update(s_self, vdn_ref[0].astype(F32), md, ld, accd)
        s_self = (jnp.sum(qf * kfn_ref[0].astype(F32), axis=-1, keepdims=True)
                  - (csum[...] + lfn_ref[0]))
        rf = self_update(s_self, vfn_ref[0].astype(F32), mf, lf, accf)

        lam = (jnp.exp(jnp.sum(lq1[...] * lk1[...], axis=-1, keepdims=True))
               - jnp.exp(jnp.sum(lq2[...] * lk2[...], axis=-1, keepdims=True)) + lam_init)
        head = (lane // 128) == (row // 2)
        coef = jnp.where(row % 2 == 0, 1.0, -lam)
        o = jnp.sum(jnp.where(head, rd * coef, 0.0), axis=0, keepdims=True)
        parts = []
        for h in range(DIFF_HEADS):
            seg = o[:, h * DIFF_V_DIM:(h + 1) * DIFF_V_DIM]
            parts.append(_rms(seg, gs_ref[...]) * (1.0 - lam_init))
        od_ref[0] = jnp.concatenate(parts, axis=-1).astype(od_ref.dtype)
        of_ref[0] = jnp.sum(jnp.where(mask_f, rf, 0.0), axis=0, keepdims=True).astype(of_ref.dtype)


def _sample_attn(page_table, qd, kdn, vdn, qf, kfn, vfn, lfn, kd, vd, kf, vf, lft,
                 bias, bself, lq1, lk1, lq2, lk2, gs, lam_init):
    ns, npages = page_table.shape
    page = kd.shape[1]
    tok = pl.BlockSpec((1, 1, 512), lambda b, p, pt: (b, 0, 0))
    cache = pl.BlockSpec((None, page, 512), lambda b, p, pt: (pt[b, p], 0, 0))
    const = lambda shape: pl.BlockSpec(shape, lambda b, p, pt: (0,) * len(shape))
    row = const((1, DIFF_HEAD_DIM))
    r3 = lambda a: a.reshape(ns, 1, 512)
    grid_spec = pltpu.PrefetchScalarGridSpec(
        num_scalar_prefetch=1,
        grid=(ns, npages),
        in_specs=[tok, tok, tok, tok, tok, tok,
                  pl.BlockSpec((1, FOX_HEADS, 1), lambda b, p, pt: (b, 0, 0)),
                  cache, cache, cache, cache,
                  pl.BlockSpec((None, FOX_HEADS, page), lambda b, p, pt: (pt[b, p], 0, 0)),
                  const((8, page)), const((8, 1)), row, row, row, row, const((1, DIFF_V_DIM))],
        out_specs=[tok, tok],
        scratch_shapes=[pltpu.VMEM((8, 1), F32), pltpu.VMEM((8, 1), F32), pltpu.VMEM((8, 512), F32),
                        pltpu.VMEM((8, 1), F32), pltpu.VMEM((8, 1), F32), pltpu.VMEM((8, 512), F32),
                        pltpu.VMEM((8, 1), F32)],
    )
    od, of = pl.pallas_call(
        functools.partial(_paged_kernel, npages, lam_init),
        grid_spec=grid_spec,
        out_shape=[jax.ShapeDtypeStruct((ns, 1, 512), BF16)] * 2,
        compiler_params=_params(("parallel", "arbitrary")),
        name="attn_paged",
    )(page_table, r3(qd), r3(kdn), r3(vdn), r3(qf), r3(kfn), r3(vfn),
      lfn.reshape(ns, FOX_HEADS, 1), kd, vd, kf, vf, lft, bias, bself,
      lq1.reshape(1, -1), lk1.reshape(1, -1), lq2.reshape(1, -1), lk2.reshape(1, -1),
      gs.reshape(1, -1))
    return od.reshape(ns, 512), of.reshape(ns, 512)


def _merge_kernel(x_ref, od_ref, of_ref, ga_ref, gb_ref, wpa_ref, wpb_ref, wo_ref,
                  gffn_ref, wpq_ref, k1_ref, k2_ref, x1_ref, hpk_ref, st_ref):
    ya = jnp.dot(od_ref[...], wpa_ref[...], preferred_element_type=F32)
    yb = jnp.dot(of_ref[...], wpb_ref[...], preferred_element_type=F32)
    t = (ga_ref[...] * ya + gb_ref[...] * yb).astype(BF16)
    x1 = x_ref[...] + jnp.dot(t, wo_ref[...], preferred_element_type=F32)
    x1_ref[...] = x1
    hb = _rms(x1, gffn_ref[...]).astype(BF16)
    hf = hb.astype(F32)
    hpk_ref[...] = _pack_pairs(hf[:, :HALF_D], hf[:, HALF_D:])
    for hm in range(2 * PEER_HEADS):
        q = jnp.dot(hb, wpq_ref[:, hm * PEER_HALF:(hm + 1) * PEER_HALF],
                    preferred_element_type=F32).astype(BF16)
        sub = k1_ref if hm % 2 == 0 else k2_ref
        st_ref[hm] = lax.dot_general(sub[...], q, NT_DIMS, preferred_element_type=F32)


def _merge(x, od, of, ga, gb, w_pa, w_pb, w_o, g_ffn, w_pq, sub_k1, sub_k2, tm):
    n = x.shape[0]
    tok = lambda w: pl.BlockSpec((tm, w), lambda i: (i, 0))
    bf = lambda a: a.astype(BF16)
    sds = jax.ShapeDtypeStruct
    return pl.pallas_call(
        _merge_kernel,
        grid=(n // tm,),
        in_specs=[tok(D_MODEL), tok(512), tok(512), tok(D_MODEL), tok(D_MODEL),
                  _const_spec(w_pa.shape), _const_spec(w_pb.shape), _const_spec(w_o.shape),
                  _const_spec((1, D_MODEL)), _const_spec(w_pq.shape),
                  _const_spec(sub_k1.shape), _const_spec(sub_k2.shape)],
        out_specs=[tok(D_MODEL), tok(HALF_D),
                   pl.BlockSpec((2 * PEER_HEADS, PEER_N_KEYS, tm), lambda i: (0, 0, i))],
        out_shape=[sds((n, D_MODEL), F32), sds((n, HALF_D), U32),
                   sds((2 * PEER_HEADS, PEER_N_KEYS, n), F32)],
        compiler_params=_params(("parallel",)),
        name="merge",
    )(x, od, of, ga, gb, bf(w_pa), bf(w_pb), bf(w_o), g_ffn.reshape(1, D_MODEL), bf(w_pq),
      bf(sub_k1), bf(sub_k2))


def _top16(x):
    rows = lax.broadcasted_iota(I32, x.shape, 0)
    vals, idxs = [], []
    for _ in range(PEER_TOPK):
        m = jnp.max(x, axis=0, keepdims=True)
        idx = jnp.min(jnp.where(x == m, rows, x.shape[0]), axis=0, keepdims=True)
        x = jnp.where(rows == idx, -jnp.inf, x)
        vals.append(m)
        idxs.append(idx)
    return jnp.concatenate(vals, axis=0), jnp.concatenate(idxs, axis=0)


def _select_rows(table, sel):
    out = jnp.zeros_like(table)
    for i in range(PEER_TOPK):
        out = out + jnp.where(sel == i, table[i:i + 1, :], 0)
    return out


def _topk_kernel(st_ref, e_ref, g_ref):
    def head(h, _):
        a, ia = _top16(st_ref[2 * h])
        b, ib = _top16(st_ref[2 * h + 1])
        cand = jnp.concatenate([a[i:i + 1, :] + b for i in range(PEER_TOPK)], axis=0)
        sc, ci = _top16(cand)
        e = _select_rows(ia, ci // PEER_TOPK) * PEER_N_KEYS + _select_rows(ib, ci % PEER_TOPK)
        ex = jnp.exp(sc - sc[0:1, :])
        g = ex / jnp.sum(ex, axis=0, keepdims=True)
        off = pl.multiple_of(h * PEER_TOPK, PEER_TOPK)
        e_ref[pl.ds(off, PEER_TOPK), :] = e
        g_ref[pl.ds(off, PEER_TOPK), :] = g
        return 0

    lax.fori_loop(0, PEER_HEADS, head, 0)


def _topk(st, tt):
    n = st.shape[2]
    spec = pl.BlockSpec((PEER_SEL, tt), lambda i: (0, i))
    return pl.pallas_call(
        _topk_kernel,
        grid=(n // tt,),
        in_specs=[pl.BlockSpec((2 * PEER_HEADS, PEER_N_KEYS, tt), lambda i: (0, 0, i))],
        out_specs=[spec, spec],
        out_shape=[jax.ShapeDtypeStruct((PEER_SEL, n), I32),
                   jax.ShapeDtypeStruct((PEER_SEL, n), F32)],
        compiler_params=_params(("parallel",)),
        name="topk",
    )(st)


def _pack_table(tab):
    b = lax.bitcast_convert_type(tab.astype(BF16), jnp.uint16).astype(U32)
    return ((b[:, HALF_D:] << 16) | b[:, :HALF_D]).reshape(tab.shape[0], 1, HALF_D)


def _gelu_tanh(x):
    return 0.5 * x * (1.0 + jnp.tanh(math.sqrt(2.0 / math.pi) * (x + 0.044715 * (x * x * x))))


def _peer_u_kernel(e_ref, h_ref, g_ref, tab_ref, w_ref, p_scr):
    ones = jnp.ones((8, LANES), F32)

    def tok(t, _):
        hw = h_ref[t]
        h_lo, h_hi = _unpack_lo(hw), _unpack_hi(hw)
        for j in range(PEER_SEL):
            row = tab_ref[e_ref[t, j]]
            prod = _unpack_lo(row) * h_lo + _unpack_hi(row) * h_hi
            p_scr[j:j + 1, :] = (prod[:, 0:128] + prod[:, 128:256]
                                 + prod[:, 256:384] + prod[:, 384:512])
        d = lax.dot_general(ones, p_scr[...], NT_DIMS, preferred_element_type=F32,
                            precision=lax.Precision.HIGHEST)[0:1, :]
        w = g_ref[t] * _gelu_tanh(d)
        w_ref[t] = w.astype(BF16).astype(F32)
        return 0

    lax.fori_loop(0, h_ref.shape[0], tok, 0)


def _peer_u(e, hpk, g, tab, tb):
    n = e.shape[0]
    return pl.pallas_call(
        _peer_u_kernel,
        grid=(n // tb,),
        in_specs=[pl.BlockSpec((tb, PEER_SEL), lambda i: (i, 0), memory_space=pltpu.SMEM),
                  pl.BlockSpec((tb, 1, HALF_D), lambda i: (i, 0, 0)),
                  pl.BlockSpec((tb, 1, PEER_SEL), lambda i: (i, 0, 0)),
                  _const_spec(tab.shape)],
        out_specs=pl.BlockSpec((tb, 1, PEER_SEL), lambda i: (i, 0, 0)),
        out_shape=jax.ShapeDtypeStruct((n, 1, PEER_SEL), F32),
        scratch_shapes=[pltpu.VMEM((PEER_SEL, LANES), F32)],
        compiler_params=_params(("parallel",)),
        name="peer_u",
    )(e, hpk.reshape(n, 1, HALF_D), g.reshape(n, 1, PEER_SEL), tab)


def _peer_v_kernel(e_ref, w_ref, tab_ref, o_ref):
    nacc = 4

    def tok(t, _):
        lo = [jnp.zeros((1, HALF_D), F32) for _ in range(nacc)]
        hi = [jnp.zeros((1, HALF_D), F32) for _ in range(nacc)]
        for j in range(PEER_SEL):
            row = tab_ref[e_ref[t, j]]
            w = w_ref[t, j]
            lo[j % nacc] = lo[j % nacc] + w * _unpack_lo(row)
            hi[j % nacc] = hi[j % nacc] + w * _unpack_hi(row)
        o_ref[t] = jnp.concatenate([(lo[0] + lo[1]) + (lo[2] + lo[3]),
                                    (hi[0] + hi[1]) + (hi[2] + hi[3])], axis=-1)
        return 0

    lax.fori_loop(0, o_ref.shape[0], tok, 0)


def _peer_v(e, w, tab, tb):
    n = e.shape[0]
    smem = pl.BlockSpec((tb, PEER_SEL), lambda i: (i, 0), memory_space=pltpu.SMEM)
    return pl.pallas_call(
        _peer_v_kernel,
        grid=(n // tb,),
        in_specs=[smem, smem, _const_spec(tab.shape)],
        out_specs=pl.BlockSpec((tb, 1, D_MODEL), lambda i: (i, 0, 0)),
        out_shape=jax.ShapeDtypeStruct((n, 1, D_MODEL), F32),
        compiler_params=_params(("parallel",)),
        name="peer_v",
    )(e, w, tab).reshape(n, D_MODEL)


def _ple_kernel(x1_ref, peer_ref, p_ref, gple_ref, wgate_ref, wproj_ref, gfin_ref, y_ref):
    x2 = x1_ref[...] + peer_ref[...]
    hn = _rms(x2, gple_ref[...]).astype(BF16)
    gate = jax.nn.sigmoid(jnp.dot(hn, wgate_ref[...], preferred_element_type=F32))
    pp = jnp.dot(p_ref[...].astype(BF16), wproj_ref[...], preferred_element_type=F32)
    y_ref[...] = _rms(x2 + gate * pp, gfin_ref[...])


def _ple(x1, peer, p, g_ple, w_gate, w_proj, g_final, tm):
    n = x1.shape[0]
    tok = lambda w: pl.BlockSpec((tm, w), lambda i: (i, 0))
    return pl.pallas_call(
        _ple_kernel,
        grid=(n // tm,),
        in_specs=[tok(D_MODEL), tok(D_MODEL), tok(PLE_DIM), _const_spec((1, D_MODEL)),
                  _const_spec(w_gate.shape), _const_spec(w_proj.shape), _const_spec((1, D_MODEL))],
        out_specs=tok(D_MODEL),
        out_shape=jax.ShapeDtypeStruct((n, D_MODEL), F32),
        compiler_params=_params(("parallel",)),
        name="ple",
    )(x1, peer, p, g_ple.reshape(1, D_MODEL), w_gate.astype(BF16), w_proj.astype(BF16),
      g_final.reshape(1, D_MODEL))


def _channel(x, od, of, ga, gb, p, lw, u_tab, v_tab, g_final, tm, tt, tb):
    x1, hpk, st = _merge(x, od, of, ga, gb, lw["w_pa"], lw["w_pb"], lw["w_o"], lw["g_ffn"],
                         lw["w_pq"], lw["sub_k1"], lw["sub_k2"], tm)
    e_t, g_t = _topk(st, tt)
    e, g = e_t.T, g_t.T
    tb = min(tb, e.shape[0])
    w = _peer_u(e, hpk, g, u_tab, tb).reshape(e.shape)
    peer = _peer_v(e, w, v_tab, tb)
    return _ple(x1, peer, p, lw["g_ple"], lw["w_ple_gate"], lw["w_ple_proj"], g_final, tm)


def kernel(x_prompt, x_sample, p_prompt, p_sample, cache_diff_k, cache_diff_v, cache_fox_k,
           cache_fox_v, cache_fox_logf, page_table, rel_table, g_attn, w_in, b_f, lam_q1, lam_k1,
           lam_q2, lam_k2, g_subln, w_pa, w_pb, w_o, g_ffn, w_pq, sub_k1, sub_k2, peer_u, peer_v,
           g_ple, w_ple_gate, w_ple_proj, g_final):
    batch, seq, _ = x_prompt.shape
    ns = x_sample.shape[0]
    depth = g_attn.shape[0]
    assert depth == 1 and x_sample.shape[1] == 1
    l = 0
    lam_init = 0.8 - 0.6 * math.exp(-0.3 * l)
    npool, page = cache_diff_k.shape[1], cache_diff_k.shape[2]
    past = page_table.shape[1] * page
    t_attn = 256
    lw = dict(w_pa=w_pa[l], w_pb=w_pb[l], w_o=w_o[l], g_ffn=g_ffn[l], w_pq=w_pq[l],
              sub_k1=sub_k1[l], sub_k2=sub_k2[l], g_ple=g_ple[l], w_ple_gate=w_ple_gate[l],
              w_ple_proj=w_ple_proj[l])
    u_tab = _pack_table(peer_u[l])
    v_tab = _pack_table(peer_v[l])
    lams = (lam_q1[l], lam_k1[l], lam_q2[l], lam_k2[l])

    xp = x_prompt.reshape(batch * seq, D_MODEL)
    (dq, dkb, dvb, fq, fkb, fvb, dk, dv, fk, fv, lf, lft, ga, gb) = _inproj(
        xp, g_attn[l], w_in[l], b_f[l], 256)
    ct = _cumsum(lft, seq)
    r = jnp.arange(t_attn)[:, None] - jnp.arange(t_attn)[None, :]
    diag = jnp.where(r >= 0, _shifted_bias(r, rel_table), NEG_INF)
    off1 = _shifted_bias(r + t_attn, rel_table)
    bias = jnp.stack([diag, off1], axis=1)
    od = _prompt_attn(True, dq, dkb, dvb, batch, seq, t_attn, (bias, *lams, g_subln[l]), lam_init)
    of = _prompt_attn(False, fq, fkb, fvb, batch, seq, t_attn, (ct,))
    yp = _channel(xp, od, of, ga, gb, p_prompt[l].reshape(batch * seq, PLE_DIM), lw, u_tab, v_tab,
                  g_final, 256, 128, 32)
    outs_p = (dk.reshape(1, batch, seq, DIFF_HEADS, 2, DIFF_HEAD_DIM),
              dv.reshape(1, batch, seq, DIFF_HEADS, DIFF_V_DIM),
              fk.reshape(1, batch, seq, FOX_HEADS, FOX_HEAD_DIM),
              fv.reshape(1, batch, seq, FOX_HEADS, FOX_HEAD_DIM),
              lf.reshape(1, batch, seq, FOX_HEADS))

    xs = x_sample.reshape(ns, D_MODEL)
    (sdq, sdkb, sdvb, sfq, sfkb, sfvb, sdk, sdv, sfk, sfv, slf, _, sga, sgb) = _inproj(
        xs, g_attn[l], w_in[l], b_f[l], ns)
    rel_page = past - (past - page + jnp.arange(page))
    bias_pg = jnp.repeat(_shifted_bias(rel_page, rel_table), 2, axis=0)
    bias_self = jnp.repeat(_shifted_bias(jnp.zeros((1,), I32), rel_table), 2, axis=0)
    sod, sof = _sample_attn(
        page_table, sdq, sdkb, sdvb, sfq, sfkb, sfvb, slf,
        cache_diff_k[l].reshape(npool, page, 512), cache_diff_v[l].reshape(npool, page, 512),
        cache_fox_k[l].reshape(npool, page, 512), cache_fox_v[l].reshape(npool, page, 512),
        jnp.swapaxes(cache_fox_logf[l], 1, 2), bias_pg, bias_self, *lams, g_subln[l], lam_init)
    ys = _channel(xs, sod, sof, sga, sgb, p_sample[l].reshape(ns, PLE_DIM), lw, u_tab, v_tab,
                  g_final, ns, ns, 32)
    outs_s = (sdk.reshape(1, ns, 1, DIFF_HEADS, 2, DIFF_HEAD_DIM),
              sdv.reshape(1, ns, 1, DIFF_HEADS, DIFF_V_DIM),
              sfk.reshape(1, ns, 1, FOX_HEADS, FOX_HEAD_DIM),
              sfv.reshape(1, ns, 1, FOX_HEADS, FOX_HEAD_DIM),
              slf.reshape(1, ns, 1, FOX_HEADS))
    return (yp.reshape(batch, seq, D_MODEL), ys.reshape(ns, 1, D_MODEL)) + outs_p + outs_s
```

```python
import functools
import math

import jax
import jax.numpy as jnp
from jax import lax
from jax.experimental import pallas as pl
from jax.experimental.pallas import tpu as pltpu

F32 = jnp.float32
BF16 = jnp.bfloat16
U32 = jnp.uint32
I32 = jnp.int32

D_MODEL = 1024
DIFF_HEADS = 4
DIFF_HEAD_DIM = 64
DIFF_V_DIM = 128
DIFF_WIDTH = 512
FOX_HEADS = 8
FOX_HEAD_DIM = 64
FOX_WIDTH = 512
ATTN_SCALE = 0.125
REL_BUCKETS = 32
REL_MAX_DIST = 128
PEER_HEADS = 8
PEER_N_KEYS = 128
PEER_HALF = 128
PEER_TOPK = 16
PEER_SEL = PEER_HEADS * PEER_TOPK
PLE_DIM = 256
EPS = 1e-6
NEG_INF = -1e30
LANES = 128
SLAB = 4
VMEM_LIMIT = 56 * 1024 * 1024

NT_DIMS = (((1,), (1,)), ((), ()))


def _const_spec(shape):
    nd = len(shape)
    return pl.BlockSpec(shape, lambda *_: (0,) * nd, pipeline_mode=pl.Buffered(1))


def _params(sem, vmem=VMEM_LIMIT):
    return pltpu.CompilerParams(dimension_semantics=sem, vmem_limit_bytes=vmem)


def _rms(x, g):
    return x * lax.rsqrt(jnp.mean(x * x, axis=-1, keepdims=True) + EPS) * g


def _log_sigmoid(x):
    return -(jnp.maximum(-x, 0.0) + jnp.log1p(jnp.exp(-jnp.abs(x))))


def _inproj_kernel(x_ref, g_ref, wqkv_ref, wfl_ref, wflt_ref, wg_ref, bfr_ref, bfc_ref,
                   dq_ref, dkb_ref, dvb_ref, fq_ref, fkb_ref, fvb_ref,
                   dk_ref, dv_ref, fk_ref, fv_ref, lf_ref, lft_ref, ga_ref, gb_ref):
    h = _rms(x_ref[...], g_ref[...]).astype(BF16)

    def proj(c):
        return jnp.dot(h, wqkv_ref[:, c * 512:(c + 1) * 512], preferred_element_type=F32)

    dq_ref[...] = (proj(0) * ATTN_SCALE).astype(BF16)
    z = proj(1); dk_ref[...] = z; dkb_ref[...] = z.astype(BF16)
    z = proj(2); dv_ref[...] = z; dvb_ref[...] = z.astype(BF16)
    fq_ref[...] = (proj(3) * ATTN_SCALE).astype(BF16)
    z = proj(4); fk_ref[...] = z; fkb_ref[...] = z.astype(BF16)
    z = proj(5); fv_ref[...] = z; fvb_ref[...] = z.astype(BF16)

    fl = jnp.dot(h, wfl_ref[...], preferred_element_type=F32)
    lf_ref[...] = _log_sigmoid(fl[:, :FOX_HEADS] + bfr_ref[...])
    flt = lax.dot_general(wflt_ref[...], h, NT_DIMS, preferred_element_type=F32)
    lft_ref[...] = _log_sigmoid(flt + bfc_ref[...])

    ga_ref[...] = jax.nn.sigmoid(jnp.dot(h, wg_ref[:, :D_MODEL], preferred_element_type=F32))
    gb_ref[...] = jax.nn.sigmoid(jnp.dot(h, wg_ref[:, D_MODEL:], preferred_element_type=F32))


def _inproj(x, g_attn, w_in, b_f, tm):
    n = x.shape[0]
    wqkv = w_in[:, :3072].astype(BF16)
    wfl = jnp.pad(w_in[:, 3072:3080], ((0, 0), (0, LANES - FOX_HEADS))).astype(BF16)
    wflt = w_in[:, 3072:3080].T.astype(BF16)
    wg = w_in[:, 3080:].astype(BF16)
    tok = lambda w: pl.BlockSpec((tm, w), lambda i: (i, 0))
    sds = jax.ShapeDtypeStruct
    out_shape = ([sds((n, 512), BF16)] * 6 + [sds((n, 512), F32)] * 4
                 + [sds((n, FOX_HEADS), F32), sds((FOX_HEADS, n), F32)]
                 + [sds((n, D_MODEL), F32)] * 2)
    out_specs = ([tok(512)] * 10
                 + [tok(FOX_HEADS), pl.BlockSpec((FOX_HEADS, tm), lambda i: (0, i))]
                 + [tok(D_MODEL)] * 2)
    return pl.pallas_call(
        _inproj_kernel,
        grid=(n // tm,),
        in_specs=[tok(D_MODEL), _const_spec((1, D_MODEL)), _const_spec(wqkv.shape),
                  _const_spec(wfl.shape), _const_spec(wflt.shape), _const_spec(wg.shape),
                  _const_spec((1, FOX_HEADS)), _const_spec((FOX_HEADS, 1))],
        out_specs=out_specs,
        out_shape=out_shape,
        compiler_params=_params(("parallel",)),
        name="inproj",
    )(x, g_attn.reshape(1, D_MODEL), wqkv, wfl, wflt, wg,
      b_f.reshape(1, FOX_HEADS), b_f.reshape(FOX_HEADS, 1))


def _lane_prefix(x):
    lane = lax.broadcasted_iota(I32, x.shape, 1)
    sh = 1
    while sh < LANES:
        x = x + jnp.where(lane >= sh, pltpu.roll(x, sh, 1), 0.0)
        sh *= 2
    return x


def _cumsum_kernel(x_ref, o_ref):
    nchunk = x_ref.shape[1] // LANES
    local = [_lane_prefix(x_ref[:, c * LANES:(c + 1) * LANES]) for c in range(nchunk)]
    carry = jnp.zeros((x_ref.shape[0], 1), F32)
    for c in range(nchunk):
        o_ref[:, c * LANES:(c + 1) * LANES] = local[c] + carry
        carry = carry + local[c][:, LANES - 1:LANES]


def _cumsum(lft, seq):
    h, n = lft.shape
    return pl.pallas_call(
        _cumsum_kernel,
        grid=(n // seq,),
        in_specs=[pl.BlockSpec((h, seq), lambda b: (0, b))],
        out_specs=pl.BlockSpec((h, seq), lambda b: (0, b)),
        out_shape=jax.ShapeDtypeStruct((h, n), F32),
        compiler_params=_params(("parallel",)),
        name="cumsum",
    )(lft)


def _t5_bucket(rel):
    n = jnp.maximum(rel, 0)
    max_exact = REL_BUCKETS // 2
    nf = jnp.maximum(n, 1).astype(F32)
    large = max_exact + (jnp.log(nf / max_exact) / math.log(REL_MAX_DIST / max_exact)
                         * (REL_BUCKETS - max_exact)).astype(I32)
    large = jnp.minimum(large, REL_BUCKETS - 1)
    return jnp.where(n < max_exact, n, large)


def _far_bias(rel_table):
    return rel_table[_t5_bucket(jnp.array(REL_MAX_DIST, I32))].astype(F32)


def _shifted_bias(rel, rel_table):
    b = rel_table[_t5_bucket(rel)].astype(F32)
    return jnp.moveaxis(b - _far_bias(rel_table), -1, 0)


def _softmax_step(s, v, m, l, acc):
    m_new = jnp.maximum(m, jnp.max(s, axis=-1, keepdims=True))
    alpha = jnp.exp(m - m_new)
    p = jnp.exp(s - m_new)
    l = alpha * l + jnp.sum(p, axis=-1, keepdims=True)
    acc = alpha * acc + jnp.dot(p.astype(BF16), v, preferred_element_type=F32)
    return m_new, l, acc


def _attn_kernel(diff, t, lam_init, *refs):
    if diff:
        q_ref, k_ref, v_ref, bias_ref, lq1, lk1, lq2, lk2, gs_ref, o_ref = refs
    else:
        q_ref, k_ref, v_ref, c_ref, o_ref = refs
    i = pl.program_id(2)
    lane = lax.broadcasted_iota(I32, (t, LANES), 1)
    q = q_ref[...].astype(F32)
    qa = jnp.where(lane < 64, q, 0.0).astype(BF16)
    qb = jnp.where(lane >= 64, q, 0.0).astype(BF16)

    def step(kb, carry, bias_a, bias_b):
        off = pl.multiple_of(kb * t, t)
        k = k_ref[pl.ds(off, t), :]
        v = v_ref[pl.ds(off, t), :]
        sa = lax.dot_general(qa, k, NT_DIMS, preferred_element_type=F32)
        sb = lax.dot_general(qb, k, NT_DIMS, preferred_element_type=F32)
        if not diff:
            sa = sa - c_ref[0:1, pl.ds(off, t)]
            sb = sb - c_ref[1:2, pl.ds(off, t)]
        if bias_a is not None:
            sa = sa + bias_a
            sb = sb + bias_b
        ma, la, acca, mb, lb, accb = carry
        ma, la, acca = _softmax_step(sa, v, ma, la, acca)
        mb, lb, accb = _softmax_step(sb, v, mb, lb, accb)
        return ma, la, acca, mb, lb, accb

    col0 = jnp.full((t, 1), NEG_INF, F32)
    zero1 = jnp.zeros((t, 1), F32)
    zacc = jnp.zeros((t, LANES), F32)
    carry = (col0, zero1, zacc, col0, zero1, zacc)

    if diff:
        n_plain = jnp.maximum(i - 1, 0)
    else:
        n_plain = i
    carry = lax.fori_loop(0, n_plain, lambda kb, c: step(kb, c, None, None), carry)
    if diff:
        off1 = bias_ref[1]
        carry = lax.cond(i >= 1, lambda c: step(i - 1, c, off1, off1), lambda c: c, carry)
        diag = bias_ref[0]
    else:
        r = lax.broadcasted_iota(I32, (t, t), 0)
        c = lax.broadcasted_iota(I32, (t, t), 1)
        diag = jnp.where(r >= c, 0.0, NEG_INF)
    ma, la, acca, mb, lb, accb = step(i, carry, diag, diag)

    if diff:
        lam = (jnp.exp(jnp.sum(lq1[...] * lk1[...], axis=-1, keepdims=True))
               - jnp.exp(jnp.sum(lq2[...] * lk2[...], axis=-1, keepdims=True)) + lam_init)
        o = acca / la - lam * (accb / lb)
        o_ref[...] = (_rms(o, gs_ref[...]) * (1.0 - lam_init)).astype(o_ref.dtype)
    else:
        o_ref[...] = jnp.where(lane < 64, acca / la, accb / lb).astype(o_ref.dtype)


def _prompt_attn(diff, q, k, v, batch, seq, t, extra, lam_init=0.0):
    nslab = 512 // LANES
    nq = seq // t
    qspec = pl.BlockSpec((t, LANES), lambda b, j, i: (b * nq + i, j))
    kvspec = pl.BlockSpec((seq, LANES), lambda b, j, i: (b, j))
    if diff:
        bias, lq1, lk1, lq2, lk2, gs = extra
        row = _const_spec((1, DIFF_HEAD_DIM))
        especs = [pl.BlockSpec((None, 2, t, t), lambda b, j, i: (j, 0, 0, 0)),
                  row, row, row, row, _const_spec((1, DIFF_V_DIM))]
        eargs = [bias, lq1.reshape(1, -1), lk1.reshape(1, -1), lq2.reshape(1, -1),
                 lk2.reshape(1, -1), gs.reshape(1, -1)]
    else:
        (ct,) = extra
        especs = [pl.BlockSpec((None, 2, seq), lambda b, j, i: (j, 0, b))]
        eargs = [ct.reshape(nslab, 2, batch * seq)]
    return pl.pallas_call(
        functools.partial(_attn_kernel, diff, t, lam_init),
        grid=(batch, nslab, nq),
        in_specs=[qspec, kvspec, kvspec] + especs,
        out_specs=qspec,
        out_shape=jax.ShapeDtypeStruct((batch * seq, 512), BF16),
        compiler_params=_params(("parallel", "parallel", "arbitrary")),
        name="attn_diff" if diff else "attn_fox",
    )(q, k, v, *eargs)


PAGES_PER_STEP = 8


def _page_prefix_kernel(x_ref, o_ref):
    x = x_ref[...]
    lane = lax.broadcasted_iota(I32, x.shape, 1)
    sh = 1
    while sh < x.shape[1]:
        x = x + jnp.where(lane >= sh, pltpu.roll(x, sh, 1), 0.0)
        sh *= 2
    o_ref[...] = x


def _page_prefix(lft_rows, rows_per_step):
    n, page = lft_rows.shape
    rows_per_step = math.gcd(n, rows_per_step)
    spec = pl.BlockSpec((rows_per_step, page), lambda i: (i, 0))
    return pl.pallas_call(
        _page_prefix_kernel,
        grid=(n // rows_per_step,),
        in_specs=[spec], out_specs=spec,
        out_shape=jax.ShapeDtypeStruct((n, page), F32),
        compiler_params=_params(("parallel",)),
        name="page_prefix",
    )(lft_rows)


def _paged_kernel(g, nsteps, page, lam_init, pt_ref, *refs):
    del pt_ref
    (qd_ref, kdn_ref, vdn_ref, qf_ref, kfn_ref, vfn_ref, lfn_ref) = refs[:7]
    kdt = refs[7:7 + g]
    vd = refs[7 + g:7 + 2 * g]
    kft = refs[7 + 2 * g:7 + 3 * g]
    vft = refs[7 + 3 * g:7 + 4 * g]
    lfp = refs[7 + 4 * g:7 + 5 * g]
    (bias_ref, bself_ref, lq1, lk1, lq2, lk2, gs_ref, od_ref, of_ref,
     md, ld, accd, mf, lf, accf, csum) = refs[7 + 5 * g:]
    p = pl.program_id(1)
    row = lax.broadcasted_iota(I32, (8, 512), 0)
    lane = lax.broadcasted_iota(I32, (8, 512), 1)
    own = (lane // 64) == row
    qd = jnp.where(own, qd_ref[0].astype(F32), 0.0)
    qf = jnp.where(own, qf_ref[0].astype(F32), 0.0)
    row1 = lax.broadcasted_iota(I32, (8, LANES), 0)

    @pl.when(p == 0)
    def _():
        for m_ref in (md, mf):
            m_ref[...] = jnp.full(m_ref.shape, NEG_INF, F32)
        for z_ref in (ld, lf, accd, accf, csum):
            z_ref[...] = jnp.zeros(z_ref.shape, F32)

    qd_b, qf_b = qd.astype(BF16), qf.astype(BF16)
    sd, sf = [], []
    carry = csum[...]
    for i in range(g):
        s = jnp.dot(qd_b, kdt[i][...].astype(BF16), preferred_element_type=F32)
        if i == g - 1:
            s = s + jnp.where(p == nsteps - 1, bias_ref[...], 0.0)
        sd.append(s)
        local = lfp[i][...]
        sf.append(jnp.dot(qf_b, kft[i][...].astype(BF16), preferred_element_type=F32)
                  - (local + carry))
        carry = carry + local[:, page - 1:page]
    csum[...] = carry
    sd = jnp.concatenate(sd, axis=1)
    sf = jnp.concatenate(sf, axis=1)

    def probs(s, m_ref, l_ref):
        m_new = jnp.maximum(m_ref[...], jnp.max(s, axis=-1, keepdims=True))
        alpha = jnp.exp(m_ref[...] - m_new)
        pr = jnp.exp(s - m_new)
        l_ref[...] = alpha * l_ref[...] + jnp.sum(pr, axis=-1, keepdims=True)
        m_ref[...] = m_new
        return alpha, pr.astype(BF16)

    alpha, pr = probs(sd, md, ld)
    acc = alpha * accd[...]
    for i in range(g):
        pi = pr[:, i * page:(i + 1) * page]
        for h in range(DIFF_HEADS):
            v_h = vd[i][pl.ds(h, page, stride=DIFF_HEADS), :].astype(BF16)
            r_h = jnp.dot(pi, v_h, preferred_element_type=F32)
            acc = acc + jnp.where(row1 // 2 == h, r_h, 0.0)
    accd[...] = acc

    alpha, pr = probs(sf, mf, lf)
    acc = alpha * accf[...]
    for i in range(g):
        acc = acc + lax.dot_general(pr[:, i * page:(i + 1) * page], vft[i][...].astype(BF16),
                                    NT_DIMS, preferred_element_type=F32)
    accf[...] = acc

    @pl.when(p == nsteps - 1)
    def _():
        def self_update(s, v_new, m_ref, l_ref, acc_ref):
            m_new = jnp.maximum(m_ref[...], s)
            alpha = jnp.exp(m_ref[...] - m_new)
            pr = jnp.exp(s - m_new)
            l = alpha * l_ref[...] + pr
            return (alpha * acc_ref[...] + pr * v_new) / l

        s_self = jnp.sum(qd * kdn_ref[0].astype(F32), axis=-1, keepdims=True) + bself_ref[...]
        vn = vdn_ref[0].astype(F32)
        vn8 = jnp.concatenate([vn[:, (r // 2) * DIFF_V_DIM:(r // 2 + 1) * DIFF_V_DIM]
                               for r in range(8)], axis=0)
        rd = self_update(s_self, vn8, md, ld, accd)
        s_self = (jnp.sum(qf * kfn_ref[0].astype(F32), axis=-1, keepdims=True)
                  - (csum[...] + lfn_ref[0]))
        rf = self_update(s_self, vfn_ref[0].astype(F32), mf, lf, accf)

        lam = (jnp.exp(jnp.sum(lq1[...] * lk1[...], axis=-1, keepdims=True))
               - jnp.exp(jnp.sum(lq2[...] * lk2[...], axis=-1, keepdims=True)) + lam_init)
        parts = []
        for h in range(DIFF_HEADS):
            seg = rd[2 * h:2 * h + 1, :] - lam * rd[2 * h + 1:2 * h + 2, :]
            parts.append(_rms(seg, gs_ref[...]) * (1.0 - lam_init))
        od_ref[0] = jnp.concatenate(parts, axis=-1).astype(od_ref.dtype)
        of_ref[0] = jnp.sum(jnp.where(own, rf, 0.0), axis=0, keepdims=True).astype(of_ref.dtype)


def _sample_attn(page_table, qd, kdn, vdn, qf, kfn, vfn, lfn, kdt, vd, kft, vft, lfp,
                 bias, bself, lq1, lk1, lq2, lk2, gs, lam_init):
    ns, npages = page_table.shape
    page = kdt.shape[2]
    g = math.gcd(npages, PAGES_PER_STEP)
    nsteps = npages // g
    tok = pl.BlockSpec((1, 1, 512), lambda b, p, pt: (b, 0, 0))
    const = lambda shape: pl.BlockSpec(shape, lambda b, p, pt: (0,) * len(shape))
    row = const((1, DIFF_HEAD_DIM))
    r3 = lambda a: a.reshape(ns, 1, 512)

    def pages(rows):
        return [pl.BlockSpec((None, rows, page), lambda b, p, pt, i=i: (pt[b, p * g + i], 0, 0))
                for i in range(g)]

    grid_spec = pltpu.PrefetchScalarGridSpec(
        num_scalar_prefetch=1,
        grid=(ns, nsteps),
        in_specs=([tok] * 6 + [pl.BlockSpec((1, FOX_HEADS, 1), lambda b, p, pt: (b, 0, 0))]
                  + pages(512) * 4 + pages(FOX_HEADS)
                  + [const((8, page)), const((8, 1)), row, row, row, row, const((1, DIFF_V_DIM))]),
        out_specs=[tok, tok],
        scratch_shapes=[pltpu.VMEM((8, 1), F32), pltpu.VMEM((8, 1), F32), pltpu.VMEM((8, LANES), F32),
                        pltpu.VMEM((8, 1), F32), pltpu.VMEM((8, 1), F32), pltpu.VMEM((8, 512), F32),
                        pltpu.VMEM((8, 1), F32)],
    )
    od, of = pl.pallas_call(
        functools.partial(_paged_kernel, g, nsteps, page, lam_init),
        grid_spec=grid_spec,
        out_shape=[jax.ShapeDtypeStruct((ns, 1, 512), BF16)] * 2,
        compiler_params=_params(("parallel", "arbitrary")),
        name="attn_paged",
    )(page_table, r3(qd), r3(kdn), r3(vdn), r3(qf), r3(kfn), r3(vfn),
      lfn.reshape(ns, FOX_HEADS, 1), *([kdt] * g), *([vd] * g), *([kft] * g), *([vft] * g),
      *([lfp] * g), bias, bself,
      lq1.reshape(1, -1), lk1.reshape(1, -1), lq2.reshape(1, -1), lk2.reshape(1, -1),
      gs.reshape(1, -1))
    return od.reshape(ns, 512), of.reshape(ns, 512)


def _merge_kernel(x_ref, od_ref, of_ref, ga_ref, gb_ref, wpa_ref, wpb_ref, wo_ref,
                  gffn_ref, wpq_ref, k1_ref, k2_ref, x1_ref, hb_ref, st_ref):
    ya = jnp.dot(od_ref[...], wpa_ref[...], preferred_element_type=F32)
    yb = jnp.dot(of_ref[...], wpb_ref[...], preferred_element_type=F32)
    t = (ga_ref[...] * ya + gb_ref[...] * yb).astype(BF16)
    x1 = x_ref[...] + jnp.dot(t, wo_ref[...], preferred_element_type=F32)
    x1_ref[...] = x1
    hb = _rms(x1, gffn_ref[...]).astype(BF16)
    hb_ref[...] = hb
    for hm in range(2 * PEER_HEADS):
        q = jnp.dot(hb, wpq_ref[:, hm * PEER_HALF:(hm + 1) * PEER_HALF],
                    preferred_element_type=F32).astype(BF16)
        sub = k1_ref if hm % 2 == 0 else k2_ref
        st_ref[hm] = lax.dot_general(sub[...], q, NT_DIMS, preferred_element_type=F32)


def _merge(x, od, of, ga, gb, w_pa, w_pb, w_o, g_ffn, w_pq, sub_k1, sub_k2, tm):
    n = x.shape[0]
    tok = lambda w: pl.BlockSpec((tm, w), lambda i: (i, 0))
    bf = lambda a: a.astype(BF16)
    sds = jax.ShapeDtypeStruct
    return pl.pallas_call(
        _merge_kernel,
        grid=(n // tm,),
        in_specs=[tok(D_MODEL), tok(512), tok(512), tok(D_MODEL), tok(D_MODEL),
                  _const_spec(w_pa.shape), _const_spec(w_pb.shape), _const_spec(w_o.shape),
                  _const_spec((1, D_MODEL)), _const_spec(w_pq.shape),
                  _const_spec(sub_k1.shape), _const_spec(sub_k2.shape)],
        out_specs=[tok(D_MODEL), tok(D_MODEL),
                   pl.BlockSpec((2 * PEER_HEADS, PEER_N_KEYS, tm), lambda i: (0, 0, i))],
        out_shape=[sds((n, D_MODEL), F32), sds((n, D_MODEL), BF16),
                   sds((2 * PEER_HEADS, PEER_N_KEYS, n), F32)],
        compiler_params=_params(("parallel",)),
        name="merge",
    )(x, od, of, ga, gb, bf(w_pa), bf(w_pb), bf(w_o), g_ffn.reshape(1, D_MODEL), bf(w_pq),
      bf(sub_k1), bf(sub_k2))


def _top16(x):
    rows = lax.broadcasted_iota(I32, x.shape, 0)
    vals, idxs = [], []
    for _ in range(PEER_TOPK):
        m = jnp.max(x, axis=0, keepdims=True)
        idx = jnp.min(jnp.where(x == m, rows, x.shape[0]), axis=0, keepdims=True)
        x = jnp.where(rows == idx, -jnp.inf, x)
        vals.append(m)
        idxs.append(idx)
    return jnp.concatenate(vals, axis=0), jnp.concatenate(idxs, axis=0)


def _select_rows(table, sel):
    out = jnp.zeros_like(table)
    for i in range(PEER_TOPK):
        out = out + jnp.where(sel == i, table[i:i + 1, :], 0)
    return out


def _topk_kernel(st_ref, e_ref, g_ref):
    def head(h, _):
        a, ia = _top16(st_ref[2 * h])
        b, ib = _top16(st_ref[2 * h + 1])
        cand = jnp.concatenate([a[i:i + 1, :] + b for i in range(PEER_TOPK)], axis=0)
        sc, ci = _top16(cand)
        e = _select_rows(ia, ci // PEER_TOPK) * PEER_N_KEYS + _select_rows(ib, ci % PEER_TOPK)
        ex = jnp.exp(sc - sc[0:1, :])
        g = ex / jnp.sum(ex, axis=0, keepdims=True)
        off = pl.multiple_of(h * PEER_TOPK, PEER_TOPK)
        e_ref[pl.ds(off, PEER_TOPK), :] = e * SLAB
        g_ref[pl.ds(off, PEER_TOPK), :] = g
        return 0

    lax.fori_loop(0, PEER_HEADS, head, 0)


def _topk(st, tt):
    n = st.shape[2]
    spec = pl.BlockSpec((PEER_SEL, tt), lambda i: (0, i))
    return pl.pallas_call(
        _topk_kernel,
        grid=(n // tt,),
        in_specs=[pl.BlockSpec((2 * PEER_HEADS, PEER_N_KEYS, tt), lambda i: (0, 0, i))],
        out_specs=[spec, spec],
        out_shape=[jax.ShapeDtypeStruct((PEER_SEL, n), I32),
                   jax.ShapeDtypeStruct((PEER_SEL, n), F32)],
        compiler_params=_params(("parallel",)),
        name="topk",
    )(st)


GROWS = PEER_SEL * SLAB


def _pack_table(tab):
    n = tab.shape[0]
    b = lax.bitcast_convert_type(tab.astype(BF16), jnp.uint16).astype(U32)
    b = b.reshape(n, SLAB, 2, LANES)
    return ((b[:, :, 1] << 16) | b[:, :, 0]).reshape(n * SLAB, LANES)


def _gelu_tanh(x):
    return 0.5 * x * (1.0 + jnp.tanh(math.sqrt(2.0 / math.pi) * (x + 0.044715 * (x * x * x))))


def _gather_experts(e_ref, t, tab_ref, graw):
    for j in range(PEER_SEL):
        r = pl.multiple_of(e_ref[t, j], SLAB)
        graw[SLAB * j:SLAB * (j + 1), :] = tab_ref[pl.ds(r, SLAB), :]


TOK_GROUP = 8


def _token_groups(ntok, gather, group_begin, compute, group_end, bufs):
    gather(0, bufs[0])

    def group(i, _):
        base = pl.multiple_of(i * TOK_GROUP, TOK_GROUP)
        ctx = group_begin(base)
        acc = None
        for r in range(TOK_GROUP):
            gather(jnp.minimum(base + r + 1, ntok - 1), bufs[(r + 1) % 2])
            acc = compute(base, r, bufs[r % 2], ctx, acc)
        group_end(base, acc)
        return 0

    lax.fori_loop(0, ntok // TOK_GROUP, group, 0)


def _group_diag():
    row = lax.broadcasted_iota(I32, (8, 2 * GROWS), 0)
    lane = lax.broadcasted_iota(I32, (8, 2 * GROWS), 1)
    return row, (lane % 8) == row


def _peer_u_kernel(e_ref, h_ref, g_ref, tab_ref, gsum_ref, w_ref, graw0, graw1):
    row, diag = _group_diag()

    def compute(base, r, graw, ctx, acc):
        b = pltpu.bitcast(graw[...], BF16)
        tt = lax.dot_general(h_ref[base + r], b, NT_DIMS, preferred_element_type=F32)
        t1 = jnp.sum(jnp.where(diag, tt, 0.0), axis=0, keepdims=True)
        placed = jnp.where(row == r, t1, 0.0)
        return placed if acc is None else acc + placed

    def group_end(base, t8):
        hi = t8.astype(BF16)
        r1 = t8 - hi.astype(F32)
        mid = r1.astype(BF16)
        lo = (r1 - mid.astype(F32)).astype(BF16)
        dd = jnp.dot(jnp.concatenate([hi, mid, lo], axis=0), gsum_ref[...],
                     preferred_element_type=F32)
        d = dd[0:8] + dd[8:16] + dd[16:24]
        w = g_ref[pl.ds(base, TOK_GROUP), :] * _gelu_tanh(d)
        w_ref[pl.ds(base, TOK_GROUP), :] = w.astype(BF16).astype(F32)

    gather = lambda t, graw: _gather_experts(e_ref, t, tab_ref, graw)
    _token_groups(h_ref.shape[0], gather, lambda base: None, compute, group_end, (graw0, graw1))


def _peer_u(e4, h3, g, tab, tb):
    n = e4.shape[0]
    j = jnp.arange(2 * GROWS)
    gsum = (j[:, None] // 8 == jnp.arange(PEER_SEL)[None, :]).astype(BF16)
    tok = pl.BlockSpec((tb, PEER_SEL), lambda i: (i, 0))
    return pl.pallas_call(
        _peer_u_kernel,
        grid=(n // tb,),
        in_specs=[pl.BlockSpec((tb, PEER_SEL), lambda i: (i, 0), memory_space=pltpu.SMEM),
                  pl.BlockSpec((tb, 8, LANES), lambda i: (i, 0, 0)), tok,
                  _const_spec(tab.shape), _const_spec(gsum.shape)],
        out_specs=tok,
        out_shape=jax.ShapeDtypeStruct((n, PEER_SEL), F32),
        scratch_shapes=[pltpu.VMEM((GROWS, LANES), U32)] * 2,
        compiler_params=_params(("parallel",)),
        name="peer_u",
    )(e4, h3, g, tab, gsum)


def _peer_v_kernel(e_ref, w_ref, tab_ref, expand_ref, o_ref, graw0, graw1):
    _, diag = _group_diag()

    def group_begin(base):
        w8 = w_ref[pl.ds(base, TOK_GROUP), :].astype(BF16)
        return jnp.dot(w8, expand_ref[...], preferred_element_type=F32)

    def compute(base, r, graw, wide, acc):
        b = pltpu.bitcast(graw[...], BF16)
        lhs = jnp.where(diag, jnp.broadcast_to(wide[r:r + 1, :], diag.shape), 0.0).astype(BF16)
        o_ref[base + r] = jnp.dot(lhs, b, preferred_element_type=F32)
        return acc

    gather = lambda t, graw: _gather_experts(e_ref, t, tab_ref, graw)
    _token_groups(o_ref.shape[0], gather, group_begin, compute, lambda base, acc: None,
                  (graw0, graw1))


def _peer_v(e4, w, tab, tb):
    n = e4.shape[0]
    j = jnp.arange(2 * GROWS)
    expand = (jnp.arange(PEER_SEL)[:, None] == j[None, :] // 8).astype(BF16)
    return pl.pallas_call(
        _peer_v_kernel,
        grid=(n // tb,),
        in_specs=[pl.BlockSpec((tb, PEER_SEL), lambda i: (i, 0), memory_space=pltpu.SMEM),
                  pl.BlockSpec((tb, PEER_SEL), lambda i: (i, 0)),
                  _const_spec(tab.shape), _const_spec(expand.shape)],
        out_specs=pl.BlockSpec((tb, 8, LANES), lambda i: (i, 0, 0)),
        out_shape=jax.ShapeDtypeStruct((n, 8, LANES), F32),
        scratch_shapes=[pltpu.VMEM((GROWS, LANES), U32)] * 2,
        compiler_params=_params(("parallel",)),
        name="peer_v",
    )(e4, w, tab, expand).reshape(n, D_MODEL)


def _ple_kernel(x1_ref, peer_ref, p_ref, gple_ref, wgate_ref, wproj_ref, gfin_ref, y_ref):
    x2 = x1_ref[...] + peer_ref[...]
    hn = _rms(x2, gple_ref[...]).astype(BF16)
    gate = jax.nn.sigmoid(jnp.dot(hn, wgate_ref[...], preferred_element_type=F32))
    pp = jnp.dot(p_ref[...].astype(BF16), wproj_ref[...], preferred_element_type=F32)
    y_ref[...] = _rms(x2 + gate * pp, gfin_ref[...])


def _ple(x1, peer, p, g_ple, w_gate, w_proj, g_final, tm):
    n = x1.shape[0]
    tok = lambda w: pl.BlockSpec((tm, w), lambda i: (i, 0))
    return pl.pallas_call(
        _ple_kernel,
        grid=(n // tm,),
        in_specs=[tok(D_MODEL), tok(D_MODEL), tok(PLE_DIM), _const_spec((1, D_MODEL)),
                  _const_spec(w_gate.shape), _const_spec(w_proj.shape), _const_spec((1, D_MODEL))],
        out_specs=tok(D_MODEL),
        out_shape=jax.ShapeDtypeStruct((n, D_MODEL), F32),
        compiler_params=_params(("parallel",)),
        name="ple",
    )(x1, peer, p, g_ple.reshape(1, D_MODEL), w_gate.astype(BF16), w_proj.astype(BF16),
      g_final.reshape(1, D_MODEL))


def _channel(x, od, of, ga, gb, p, lw, u_tab, v_tab, g_final, tm, tt, tb):
    n = x.shape[0]
    x1, hb, st = _merge(x, od, of, ga, gb, lw["w_pa"], lw["w_pb"], lw["w_o"], lw["g_ffn"],
                        lw["w_pq"], lw["sub_k1"], lw["sub_k2"], tm)
    e_t, g_t = _topk(st, tt)
    e4, g = e_t.T, g_t.T
    tb = min(tb, n)
    w = _peer_u(e4, hb.reshape(n, 8, LANES), g, u_tab, tb)
    peer = _peer_v(e4, w, v_tab, tb)
    return _ple(x1, peer, p, lw["g_ple"], lw["w_ple_gate"], lw["w_ple_proj"], g_final, tm)


def kernel(x_prompt, x_sample, p_prompt, p_sample, cache_diff_k, cache_diff_v, cache_fox_k,
           cache_fox_v, cache_fox_logf, page_table, rel_table, g_attn, w_in, b_f, lam_q1, lam_k1,
           lam_q2, lam_k2, g_subln, w_pa, w_pb, w_o, g_ffn, w_pq, sub_k1, sub_k2, peer_u, peer_v,
           g_ple, w_ple_gate, w_ple_proj, g_final):
    batch, seq, _ = x_prompt.shape
    ns = x_sample.shape[0]
    depth = g_attn.shape[0]
    assert depth == 1 and x_sample.shape[1] == 1
    l = 0
    lam_init = 0.8 - 0.6 * math.exp(-0.3 * l)
    npool, page = cache_diff_k.shape[1], cache_diff_k.shape[2]
    past = page_table.shape[1] * page
    t_attn = 256
    lw = dict(w_pa=w_pa[l], w_pb=w_pb[l], w_o=w_o[l], g_ffn=g_ffn[l], w_pq=w_pq[l],
              sub_k1=sub_k1[l], sub_k2=sub_k2[l], g_ple=g_ple[l], w_ple_gate=w_ple_gate[l],
              w_ple_proj=w_ple_proj[l])
    u_tab = _pack_table(peer_u[l])
    v_tab = _pack_table(peer_v[l])
    lams = (lam_q1[l], lam_k1[l], lam_q2[l], lam_k2[l])

    xp = x_prompt.reshape(batch * seq, D_MODEL)
    (dq, dkb, dvb, fq, fkb, fvb, dk, dv, fk, fv, lf, lft, ga, gb) = _inproj(
        xp, g_attn[l], w_in[l], b_f[l], 256)
    ct = _cumsum(lft, seq)
    r = jnp.arange(t_attn)[:, None] - jnp.arange(t_attn)[None, :]
    diag = jnp.where(r >= 0, _shifted_bias(r, rel_table), NEG_INF)
    off1 = _shifted_bias(r + t_attn, rel_table)
    bias = jnp.stack([diag, off1], axis=1)
    od = _prompt_attn(True, dq, dkb, dvb, batch, seq, t_attn, (bias, *lams, g_subln[l]), lam_init)
    of = _prompt_attn(False, fq, fkb, fvb, batch, seq, t_attn, (ct,))
    yp = _channel(xp, od, of, ga, gb, p_prompt[l].reshape(batch * seq, PLE_DIM), lw, u_tab, v_tab,
                  g_final, 256, 128, 64)
    outs_p = (dk.reshape(1, batch, seq, DIFF_HEADS, 2, DIFF_HEAD_DIM),
              dv.reshape(1, batch, seq, DIFF_HEADS, DIFF_V_DIM),
              fk.reshape(1, batch, seq, FOX_HEADS, FOX_HEAD_DIM),
              fv.reshape(1, batch, seq, FOX_HEADS, FOX_HEAD_DIM),
              lf.reshape(1, batch, seq, FOX_HEADS))

    xs = x_sample.reshape(ns, D_MODEL)
    (sdq, sdkb, sdvb, sfq, sfkb, sfvb, sdk, sdv, sfk, sfv, slf, _, sga, sgb) = _inproj(
        xs, g_attn[l], w_in[l], b_f[l], ns)
    rel_page = past - (past - page + jnp.arange(page))
    bias_pg = jnp.repeat(_shifted_bias(rel_page, rel_table), 2, axis=0)
    bias_self = jnp.repeat(_shifted_bias(jnp.zeros((1,), I32), rel_table), 2, axis=0)
    feat_major = lambda c: jnp.moveaxis(c.reshape(npool, page, 512), 1, 2)
    lfp = _page_prefix(jnp.swapaxes(cache_fox_logf[l], 1, 2).reshape(npool * FOX_HEADS, page), 1024)
    sod, sof = _sample_attn(
        page_table, sdq, sdkb, sdvb, sfq, sfkb, sfvb, slf,
        feat_major(cache_diff_k[l]), cache_diff_v[l].reshape(npool, page * DIFF_HEADS, DIFF_V_DIM),
        feat_major(cache_fox_k[l]), feat_major(cache_fox_v[l]),
        lfp.reshape(npool, FOX_HEADS, page), bias_pg, bias_self, *lams, g_subln[l], lam_init)
    ys = _channel(xs, sod, sof, sga, sgb, p_sample[l].reshape(ns, PLE_DIM), lw, u_tab, v_tab,
                  g_final, ns, ns, 32)
    outs_s = (sdk.reshape(1, ns, 1, DIFF_HEADS, 2, DIFF_HEAD_DIM),
              sdv.reshape(1, ns, 1, DIFF_HEADS, DIFF_V_DIM),
              sfk.reshape(1, ns, 1, FOX_HEADS, FOX_HEAD_DIM),
              sfv.reshape(1, ns, 1, FOX_HEADS, FOX_HEAD_DIM),
              slf.reshape(1, ns, 1, FOX_HEADS))
    return (yp.reshape(batch, seq, D_MODEL), ys.reshape(ns, 1, D_MODEL)) + outs_p + outs_s
```

```python
import functools
import math

import jax
import jax.numpy as jnp
from jax import lax
from jax.experimental import pallas as pl
from jax.experimental.pallas import tpu as pltpu

F32 = jnp.float32
BF16 = jnp.bfloat16
U32 = jnp.uint32
I32 = jnp.int32

D_MODEL = 1024
DIFF_HEADS = 4
DIFF_HEAD_DIM = 64
DIFF_V_DIM = 128
DIFF_WIDTH = 512
FOX_HEADS = 8
FOX_HEAD_DIM = 64
FOX_WIDTH = 512
ATTN_SCALE = 0.125
REL_BUCKETS = 32
REL_MAX_DIST = 128
PEER_HEADS = 8
PEER_N_KEYS = 128
PEER_HALF = 128
PEER_TOPK = 16
PEER_SEL = PEER_HEADS * PEER_TOPK
PLE_DIM = 256
EPS = 1e-6
NEG_INF = -1e30
LANES = 128
SLAB = 4
VMEM_LIMIT = 56 * 1024 * 1024

NT_DIMS = (((1,), (1,)), ((), ()))


def _const_spec(shape):
    nd = len(shape)
    return pl.BlockSpec(shape, lambda *_: (0,) * nd, pipeline_mode=pl.Buffered(1))


def _params(sem, vmem=VMEM_LIMIT):
    return pltpu.CompilerParams(dimension_semantics=sem, vmem_limit_bytes=vmem)


def _rms(x, g):
    return x * lax.rsqrt(jnp.mean(x * x, axis=-1, keepdims=True) + EPS) * g


def _log_sigmoid(x):
    return -(jnp.maximum(-x, 0.0) + jnp.log1p(jnp.exp(-jnp.abs(x))))


def _inproj_kernel(x_ref, g_ref, wqkv_ref, wfl_ref, wflt_ref, wg_ref, bfr_ref, bfc_ref, wvt_ref,
                   dq_ref, dkb_ref, dvb_ref, fq_ref, fkb_ref, fvb_ref,
                   dk_ref, dv_ref, fk_ref, fv_ref, lf_ref, lft_ref, ga_ref, gb_ref,
                   dvt_ref, fvt_ref):
    h = _rms(x_ref[...], g_ref[...]).astype(BF16)
    dvt_ref[...] = lax.dot_general(wvt_ref[:512, :], h, NT_DIMS,
                                   preferred_element_type=F32).astype(BF16)
    fvt_ref[...] = lax.dot_general(wvt_ref[512:, :], h, NT_DIMS,
                                   preferred_element_type=F32).astype(BF16)

    def proj(c):
        return jnp.dot(h, wqkv_ref[:, c * 512:(c + 1) * 512], preferred_element_type=F32)

    dq_ref[...] = (proj(0) * ATTN_SCALE).astype(BF16)
    z = proj(1); dk_ref[...] = z; dkb_ref[...] = z.astype(BF16)
    z = proj(2); dv_ref[...] = z; dvb_ref[...] = z.astype(BF16)
    fq_ref[...] = (proj(3) * ATTN_SCALE).astype(BF16)
    z = proj(4); fk_ref[...] = z; fkb_ref[...] = z.astype(BF16)
    z = proj(5); fv_ref[...] = z; fvb_ref[...] = z.astype(BF16)

    fl = jnp.dot(h, wfl_ref[...], preferred_element_type=F32)
    lf_ref[...] = _log_sigmoid(fl[:, :FOX_HEADS] + bfr_ref[...])
    flt = lax.dot_general(wflt_ref[...], h, NT_DIMS, preferred_element_type=F32)
    lft_ref[...] = _log_sigmoid(flt + bfc_ref[...])

    ga_ref[...] = jax.nn.sigmoid(jnp.dot(h, wg_ref[:, :D_MODEL], preferred_element_type=F32))
    gb_ref[...] = jax.nn.sigmoid(jnp.dot(h, wg_ref[:, D_MODEL:], preferred_element_type=F32))


def _inproj(x, g_attn, w_in, b_f, tm):
    n = x.shape[0]
    wqkv = w_in[:, :3072].astype(BF16)
    wfl = jnp.pad(w_in[:, 3072:3080], ((0, 0), (0, LANES - FOX_HEADS))).astype(BF16)
    wflt = w_in[:, 3072:3080].T.astype(BF16)
    wg = w_in[:, 3080:].astype(BF16)
    wvt = jnp.concatenate([w_in[:, 1024:1536], w_in[:, 2560:3072]], axis=1).T.astype(BF16)
    tok = lambda w: pl.BlockSpec((tm, w), lambda i: (i, 0))
    feat = pl.BlockSpec((512, tm), lambda i: (0, i))
    sds = jax.ShapeDtypeStruct
    out_shape = ([sds((n, 512), BF16)] * 6 + [sds((n, 512), F32)] * 4
                 + [sds((n, FOX_HEADS), F32), sds((FOX_HEADS, n), F32)]
                 + [sds((n, D_MODEL), F32)] * 2 + [sds((512, n), BF16)] * 2)
    out_specs = ([tok(512)] * 10
                 + [tok(FOX_HEADS), pl.BlockSpec((FOX_HEADS, tm), lambda i: (0, i))]
                 + [tok(D_MODEL)] * 2 + [feat, feat])
    return pl.pallas_call(
        _inproj_kernel,
        grid=(n // tm,),
        in_specs=[tok(D_MODEL), _const_spec((1, D_MODEL)), _const_spec(wqkv.shape),
                  _const_spec(wfl.shape), _const_spec(wflt.shape), _const_spec(wg.shape),
                  _const_spec((1, FOX_HEADS)), _const_spec((FOX_HEADS, 1)), _const_spec(wvt.shape)],
        out_specs=out_specs,
        out_shape=out_shape,
        compiler_params=_params(("parallel",)),
        name="inproj",
    )(x, g_attn.reshape(1, D_MODEL), wqkv, wfl, wflt, wg,
      b_f.reshape(1, FOX_HEADS), b_f.reshape(FOX_HEADS, 1), wvt)


def _lane_prefix(x):
    lane = lax.broadcasted_iota(I32, x.shape, 1)
    sh = 1
    while sh < LANES:
        x = x + jnp.where(lane >= sh, pltpu.roll(x, sh, 1), 0.0)
        sh *= 2
    return x


def _cumsum_kernel(x_ref, o_ref):
    nchunk = x_ref.shape[1] // LANES
    local = [_lane_prefix(x_ref[:, c * LANES:(c + 1) * LANES]) for c in range(nchunk)]
    carry = jnp.zeros((x_ref.shape[0], 1), F32)
    for c in range(nchunk):
        o_ref[:, c * LANES:(c + 1) * LANES] = local[c] + carry
        carry = carry + local[c][:, LANES - 1:LANES]


def _cumsum(lft, seq):
    h, n = lft.shape
    return pl.pallas_call(
        _cumsum_kernel,
        grid=(n // seq,),
        in_specs=[pl.BlockSpec((h, seq), lambda b: (0, b))],
        out_specs=pl.BlockSpec((h, seq), lambda b: (0, b)),
        out_shape=jax.ShapeDtypeStruct((h, n), F32),
        compiler_params=_params(("parallel",)),
        name="cumsum",
    )(lft)


def _t5_bucket(rel):
    n = jnp.maximum(rel, 0)
    max_exact = REL_BUCKETS // 2
    nf = jnp.maximum(n, 1).astype(F32)
    large = max_exact + (jnp.log(nf / max_exact) / math.log(REL_MAX_DIST / max_exact)
                         * (REL_BUCKETS - max_exact)).astype(I32)
    large = jnp.minimum(large, REL_BUCKETS - 1)
    return jnp.where(n < max_exact, n, large)


def _far_bias(rel_table):
    return rel_table[_t5_bucket(jnp.array(REL_MAX_DIST, I32))].astype(F32)


def _shifted_bias(rel, rel_table):
    b = rel_table[_t5_bucket(rel)].astype(F32)
    return jnp.moveaxis(b - _far_bias(rel_table), -1, 0)


def _bias_tiles(rel_table, t):
    by_dist = _shifted_bias(jnp.arange(2 * t), rel_table)
    masked = jnp.full((by_dist.shape[0], t - 1), NEG_INF, F32)
    ext = jnp.concatenate([masked, by_dist[:, :t]], axis=1)
    window = lambda vec, start: jax.vmap(
        lambda s: lax.dynamic_slice_in_dim(vec, s, t, axis=1), out_axes=1)(start)
    keys = jnp.arange(t)
    diag = window(ext, t - 1 - keys)
    off1 = window(by_dist, t - keys)
    return jnp.stack([diag, off1], axis=1)


def _softmax_step(s, vt, m, l, acc):
    m_new = jnp.maximum(m, jnp.max(s, axis=0, keepdims=True))
    alpha = jnp.exp(m - m_new)
    p = jnp.exp(s - m_new)
    l = alpha * l + jnp.sum(p, axis=0, keepdims=True)
    acc = alpha * acc + jnp.dot(vt, p.astype(BF16), preferred_element_type=F32)
    return m_new, l, acc


def _attn_kernel(diff, t, lam_init, *refs):
    if diff:
        q_ref, k_ref, vt_ref, bias_ref, lq1, lk1, lq2, lk2, gs_ref, o_ref = refs
    else:
        q_ref, k_ref, vt_ref, c_ref, o_ref = refs
    i = pl.program_id(2)
    lane = lax.broadcasted_iota(I32, (t, LANES), 1)
    q = q_ref[...].astype(F32)
    qa = jnp.where(lane < 64, q, 0.0).astype(BF16)
    qb = jnp.where(lane >= 64, q, 0.0).astype(BF16)

    def step(kb, carry, bias_a, bias_b):
        off = pl.multiple_of(kb * t, t)
        k = k_ref[pl.ds(off, t), :]
        vt = vt_ref[:, pl.ds(off, t)]
        sa = lax.dot_general(k, qa, NT_DIMS, preferred_element_type=F32)
        sb = lax.dot_general(k, qb, NT_DIMS, preferred_element_type=F32)
        if not diff:
            c = c_ref[pl.ds(off, t), :]
            sa = sa - c[:, 0:1]
            sb = sb - c[:, 1:2]
        if bias_a is not None:
            sa = sa + bias_a
            sb = sb + bias_b
        ma, la, acca, mb, lb, accb = carry
        ma, la, acca = _softmax_step(sa, vt, ma, la, acca)
        mb, lb, accb = _softmax_step(sb, vt, mb, lb, accb)
        return ma, la, acca, mb, lb, accb

    row0 = jnp.full((1, t), NEG_INF, F32)
    zero1 = jnp.zeros((1, t), F32)
    zacc = jnp.zeros((LANES, t), F32)
    carry = (row0, zero1, zacc, row0, zero1, zacc)

    if diff:
        n_plain = jnp.maximum(i - 1, 0)
    else:
        n_plain = i
    carry = lax.fori_loop(0, n_plain, lambda kb, c: step(kb, c, None, None), carry)
    if diff:
        off1 = bias_ref[1]
        carry = lax.cond(i >= 1, lambda c: step(i - 1, c, off1, off1), lambda c: c, carry)
        diag = bias_ref[0]
    else:
        key = lax.broadcasted_iota(I32, (t, t), 0)
        qry = lax.broadcasted_iota(I32, (t, t), 1)
        diag = jnp.where(qry >= key, 0.0, NEG_INF)
    ma, la, acca, mb, lb, accb = step(i, carry, diag, diag)

    if diff:
        lam = (jnp.exp(jnp.sum(lq1[...] * lk1[...], axis=-1, keepdims=True))
               - jnp.exp(jnp.sum(lq2[...] * lk2[...], axis=-1, keepdims=True)) + lam_init)
        o = acca / la - lam * (accb / lb)
        y = o * lax.rsqrt(jnp.mean(o * o, axis=0, keepdims=True) + EPS) * gs_ref[...]
        o_ref[...] = (y * (1.0 - lam_init)).T.astype(o_ref.dtype)
    else:
        feat = lax.broadcasted_iota(I32, (LANES, t), 0)
        o_ref[...] = jnp.where(feat < 64, acca / la, accb / lb).T.astype(o_ref.dtype)


def _prompt_attn(diff, q, k, vt, batch, seq, t, extra, lam_init=0.0):
    nslab = 512 // LANES
    nq = seq // t
    qspec = pl.BlockSpec((t, LANES), lambda b, j, i: (b * nq + i, j))
    kspec = pl.BlockSpec((seq, LANES), lambda b, j, i: (b, j))
    vtspec = pl.BlockSpec((LANES, seq), lambda b, j, i: (j, b))
    if diff:
        bias, lq1, lk1, lq2, lk2, gs = extra
        row = _const_spec((1, DIFF_HEAD_DIM))
        especs = [pl.BlockSpec((None, 2, t, t), lambda b, j, i: (j, 0, 0, 0)),
                  row, row, row, row, _const_spec((DIFF_V_DIM, 1))]
        eargs = [bias, lq1.reshape(1, -1), lk1.reshape(1, -1), lq2.reshape(1, -1),
                 lk2.reshape(1, -1), gs.reshape(-1, 1)]
    else:
        (c,) = extra
        especs = [pl.BlockSpec((None, seq, 2), lambda b, j, i: (j, b, 0))]
        eargs = [jnp.moveaxis(c.reshape(batch * seq, nslab, 2), 1, 0)]
    return pl.pallas_call(
        functools.partial(_attn_kernel, diff, t, lam_init),
        grid=(batch, nslab, nq),
        in_specs=[qspec, kspec, vtspec] + especs,
        out_specs=qspec,
        out_shape=jax.ShapeDtypeStruct((batch * seq, 512), BF16),
        compiler_params=_params(("parallel", "parallel", "arbitrary")),
        name="attn_diff" if diff else "attn_fox",
    )(q, k, vt, *eargs)


PAGES_PER_STEP = 8


def _page_prefix_kernel(x_ref, o_ref):
    x = x_ref[...]
    lane = lax.broadcasted_iota(I32, x.shape, 1)
    sh = 1
    while sh < x.shape[1]:
        x = x + jnp.where(lane >= sh, pltpu.roll(x, sh, 1), 0.0)
        sh *= 2
    o_ref[...] = x


def _page_prefix(lft_rows, rows_per_step):
    n, page = lft_rows.shape
    rows_per_step = math.gcd(n, rows_per_step)
    spec = pl.BlockSpec((rows_per_step, page), lambda i: (i, 0))
    return pl.pallas_call(
        _page_prefix_kernel,
        grid=(n // rows_per_step,),
        in_specs=[spec], out_specs=spec,
        out_shape=jax.ShapeDtypeStruct((n, page), F32),
        compiler_params=_params(("parallel",)),
        name="page_prefix",
    )(lft_rows)


def _paged_kernel(g, nsteps, page, lam_init, pt_ref, *refs):
    del pt_ref
    (qd_ref, kdn_ref, vdn_ref, qf_ref, kfn_ref, vfn_ref, lfn_ref) = refs[:7]
    kdt = refs[7:7 + g]
    vd = refs[7 + g:7 + 2 * g]
    kft = refs[7 + 2 * g:7 + 3 * g]
    vft = refs[7 + 3 * g:7 + 4 * g]
    lfp = refs[7 + 4 * g:7 + 5 * g]
    (bias_ref, bself_ref, lq1, lk1, lq2, lk2, gs_ref, od_ref, of_ref,
     md, ld, accd, mf, lf, accf, csum) = refs[7 + 5 * g:]
    p = pl.program_id(1)
    row = lax.broadcasted_iota(I32, (8, 512), 0)
    lane = lax.broadcasted_iota(I32, (8, 512), 1)
    own = (lane // 64) == row
    qd = jnp.where(own, qd_ref[0].astype(F32), 0.0)
    qf = jnp.where(own, qf_ref[0].astype(F32), 0.0)
    row1 = lax.broadcasted_iota(I32, (8, LANES), 0)

    @pl.when(p == 0)
    def _():
        for m_ref in (md, mf):
            m_ref[...] = jnp.full(m_ref.shape, NEG_INF, F32)
        for z_ref in (ld, lf, accd, accf, csum):
            z_ref[...] = jnp.zeros(z_ref.shape, F32)

    qd_b, qf_b = qd.astype(BF16), qf.astype(BF16)
    sd, sf = [], []
    carry = csum[...]
    for i in range(g):
        s = jnp.dot(qd_b, kdt[i][...].astype(BF16), preferred_element_type=F32)
        if i == g - 1:
            s = s + jnp.where(p == nsteps - 1, bias_ref[...], 0.0)
        sd.append(s)
        local = lfp[i][...]
        sf.append(jnp.dot(qf_b, kft[i][...].astype(BF16), preferred_element_type=F32)
                  - (local + carry))
        carry = carry + local[:, page - 1:page]
    csum[...] = carry
    sd = jnp.concatenate(sd, axis=1)
    sf = jnp.concatenate(sf, axis=1)

    def probs(s, m_ref, l_ref):
        m_new = jnp.maximum(m_ref[...], jnp.max(s, axis=-1, keepdims=True))
        alpha = jnp.exp(m_ref[...] - m_new)
        pr = jnp.exp(s - m_new)
        l_ref[...] = alpha * l_ref[...] + jnp.sum(pr, axis=-1, keepdims=True)
        m_ref[...] = m_new
        return alpha, pr.astype(BF16)

    alpha, pr = probs(sd, md, ld)
    acc = alpha * accd[...]
    for i in range(g):
        pi = pr[:, i * page:(i + 1) * page]
        for h in range(DIFF_HEADS):
            v_h = vd[i][pl.ds(h, page, stride=DIFF_HEADS), :].astype(BF16)
            r_h = jnp.dot(pi, v_h, preferred_element_type=F32)
            acc = acc + jnp.where(row1 // 2 == h, r_h, 0.0)
    accd[...] = acc

    alpha, pr = probs(sf, mf, lf)
    acc = alpha * accf[...]
    for i in range(g):
        acc = acc + lax.dot_general(pr[:, i * page:(i + 1) * page], vft[i][...].astype(BF16),
                                    NT_DIMS, preferred_element_type=F32)
    accf[...] = acc

    @pl.when(p == nsteps - 1)
    def _():
        def self_update(s, v_new, m_ref, l_ref, acc_ref):
            m_new = jnp.maximum(m_ref[...], s)
            alpha = jnp.exp(m_ref[...] - m_new)
            pr = jnp.exp(s - m_new)
            l = alpha * l_ref[...] + pr
            return (alpha * acc_ref[...] + pr * v_new) / l

        s_self = jnp.sum(qd * kdn_ref[0].astype(F32), axis=-1, keepdims=True) + bself_ref[...]
        vn = vdn_ref[0].astype(F32)
        vn8 = jnp.concatenate([vn[:, (r // 2) * DIFF_V_DIM:(r // 2 + 1) * DIFF_V_DIM]
                               for r in range(8)], axis=0)
        rd = self_update(s_self, vn8, md, ld, accd)
        s_self = (jnp.sum(qf * kfn_ref[0].astype(F32), axis=-1, keepdims=True)
                  - (csum[...] + lfn_ref[0]))
        rf = self_update(s_self, vfn_ref[0].astype(F32), mf, lf, accf)

        lam = (jnp.exp(jnp.sum(lq1[...] * lk1[...], axis=-1, keepdims=True))
               - jnp.exp(jnp.sum(lq2[...] * lk2[...], axis=-1, keepdims=True)) + lam_init)
        parts = []
        for h in range(DIFF_HEADS):
            seg = rd[2 * h:2 * h + 1, :] - lam * rd[2 * h + 1:2 * h + 2, :]
            parts.append(_rms(seg, gs_ref[...]) * (1.0 - lam_init))
        od_ref[0] = jnp.concatenate(parts, axis=-1).astype(od_ref.dtype)
        of_ref[0] = jnp.sum(jnp.where(own, rf, 0.0), axis=0, keepdims=True).astype(of_ref.dtype)


def _sample_attn(page_table, qd, kdn, vdn, qf, kfn, vfn, lfn, kdt, vd, kft, vft, lfp,
                 bias, bself, lq1, lk1, lq2, lk2, gs, lam_init):
    ns, npages = page_table.shape
    page = kdt.shape[2]
    g = math.gcd(npages, PAGES_PER_STEP)
    nsteps = npages // g
    tok = pl.BlockSpec((1, 1, 512), lambda b, p, pt: (b, 0, 0))
    const = lambda shape: pl.BlockSpec(shape, lambda b, p, pt: (0,) * len(shape))
    row = const((1, DIFF_HEAD_DIM))
    r3 = lambda a: a.reshape(ns, 1, 512)

    def pages(rows):
        return [pl.BlockSpec((None, rows, page), lambda b, p, pt, i=i: (pt[b, p * g + i], 0, 0))
                for i in range(g)]

    grid_spec = pltpu.PrefetchScalarGridSpec(
        num_scalar_prefetch=1,
        grid=(ns, nsteps),
        in_specs=([tok] * 6 + [pl.BlockSpec((1, FOX_HEADS, 1), lambda b, p, pt: (b, 0, 0))]
                  + pages(512) * 4 + pages(FOX_HEADS)
                  + [const((8, page)), const((8, 1)), row, row, row, row, const((1, DIFF_V_DIM))]),
        out_specs=[tok, tok],
        scratch_shapes=[pltpu.VMEM((8, 1), F32), pltpu.VMEM((8, 1), F32), pltpu.VMEM((8, LANES), F32),
                        pltpu.VMEM((8, 1), F32), pltpu.VMEM((8, 1), F32), pltpu.VMEM((8, 512), F32),
                        pltpu.VMEM((8, 1), F32)],
    )
    od, of = pl.pallas_call(
        functools.partial(_paged_kernel, g, nsteps, page, lam_init),
        grid_spec=grid_spec,
        out_shape=[jax.ShapeDtypeStruct((ns, 1, 512), BF16)] * 2,
        compiler_params=_params(("parallel", "arbitrary")),
        name="attn_paged",
    )(page_table, r3(qd), r3(kdn), r3(vdn), r3(qf), r3(kfn), r3(vfn),
      lfn.reshape(ns, FOX_HEADS, 1), *([kdt] * g), *([vd] * g), *([kft] * g), *([vft] * g),
      *([lfp] * g), bias, bself,
      lq1.reshape(1, -1), lk1.reshape(1, -1), lq2.reshape(1, -1), lk2.reshape(1, -1),
      gs.reshape(1, -1))
    return od.reshape(ns, 512), of.reshape(ns, 512)


def _merge_kernel(x_ref, od_ref, of_ref, ga_ref, gb_ref, wpa_ref, wpb_ref, wo_ref,
                  gffn_ref, wpq_ref, k1_ref, k2_ref, x1_ref, hb_ref, st_ref):
    ya = jnp.dot(od_ref[...], wpa_ref[...], preferred_element_type=F32)
    yb = jnp.dot(of_ref[...], wpb_ref[...], preferred_element_type=F32)
    t = (ga_ref[...] * ya + gb_ref[...] * yb).astype(BF16)
    x1 = x_ref[...] + jnp.dot(t, wo_ref[...], preferred_element_type=F32)
    x1_ref[...] = x1
    hb = _rms(x1, gffn_ref[...]).astype(BF16)
    hb_ref[...] = hb
    for hm in range(2 * PEER_HEADS):
        q = jnp.dot(hb, wpq_ref[:, hm * PEER_HALF:(hm + 1) * PEER_HALF],
                    preferred_element_type=F32).astype(BF16)
        sub = k1_ref if hm % 2 == 0 else k2_ref
        st_ref[hm] = lax.dot_general(sub[...], q, NT_DIMS, preferred_element_type=F32)


def _merge(x, od, of, ga, gb, w_pa, w_pb, w_o, g_ffn, w_pq, sub_k1, sub_k2, tm):
    n = x.shape[0]
    tok = lambda w: pl.BlockSpec((tm, w), lambda i: (i, 0))
    bf = lambda a: a.astype(BF16)
    sds = jax.ShapeDtypeStruct
    return pl.pallas_call(
        _merge_kernel,
        grid=(n // tm,),
        in_specs=[tok(D_MODEL), tok(512), tok(512), tok(D_MODEL), tok(D_MODEL),
                  _const_spec(w_pa.shape), _const_spec(w_pb.shape), _const_spec(w_o.shape),
                  _const_spec((1, D_MODEL)), _const_spec(w_pq.shape),
                  _const_spec(sub_k1.shape), _const_spec(sub_k2.shape)],
        out_specs=[tok(D_MODEL), tok(D_MODEL),
                   pl.BlockSpec((2 * PEER_HEADS, PEER_N_KEYS, tm), lambda i: (0, 0, i))],
        out_shape=[sds((n, D_MODEL), F32), sds((n, D_MODEL), BF16),
                   sds((2 * PEER_HEADS, PEER_N_KEYS, n), F32)],
        compiler_params=_params(("parallel",)),
        name="merge",
    )(x, od, of, ga, gb, bf(w_pa), bf(w_pb), bf(w_o), g_ffn.reshape(1, D_MODEL), bf(w_pq),
      bf(sub_k1), bf(sub_k2))


def _top16(x):
    rows = lax.broadcasted_iota(I32, x.shape, 0)
    vals, idxs = [], []
    for _ in range(PEER_TOPK):
        m = jnp.max(x, axis=0, keepdims=True)
        idx = jnp.min(jnp.where(x == m, rows, x.shape[0]), axis=0, keepdims=True)
        x = jnp.where(rows == idx, -jnp.inf, x)
        vals.append(m)
        idxs.append(idx)
    return jnp.concatenate(vals, axis=0), jnp.concatenate(idxs, axis=0)


def _top16_of_sums(a, b):
    sub = lax.broadcasted_iota(I32, (8, a.shape[1]), 0)
    pieces = [a[0:1, :] + b]
    for i in range(1, 8):
        piece = a[i:i + 1, :] + b[0:8, :]
        live = PEER_TOPK // (i + 1)
        pieces.append(piece if live >= 8 else jnp.where(sub < live, piece, -jnp.inf))
    pieces.append(a[8:16, :] + b[0:1, :])
    sc, ridx = _top16(jnp.concatenate(pieces, axis=0))
    blk, low = ridx >> 3, ridx & 7
    i_sel = jnp.where(blk <= 1, 0, jnp.where(blk == 9, 8 + low, blk - 1))
    j_sel = jnp.where(blk <= 1, ridx, jnp.where(blk == 9, 0, low))
    return sc, i_sel, j_sel


def _select_rows(table, sel):
    out = jnp.zeros_like(table)
    for i in range(PEER_TOPK):
        out = out + jnp.where(sel == i, table[i:i + 1, :], 0)
    return out


def _topk_kernel(st_ref, e_ref, g_ref):
    def head(h, _):
        a, ia = _top16(st_ref[2 * h])
        b, ib = _top16(st_ref[2 * h + 1])
        sc, i_sel, j_sel = _top16_of_sums(a, b)
        e = _select_rows(ia, i_sel) * PEER_N_KEYS + _select_rows(ib, j_sel)
        ex = jnp.exp(sc - sc[0:1, :])
        g = ex / jnp.sum(ex, axis=0, keepdims=True)
        off = pl.multiple_of(h * PEER_TOPK, PEER_TOPK)
        e_ref[pl.ds(off, PEER_TOPK), :] = e * SLAB
        g_ref[pl.ds(off, PEER_TOPK), :] = g
        return 0

    lax.fori_loop(0, PEER_HEADS, head, 0)


def _topk(st, tt):
    n = st.shape[2]
    spec = pl.BlockSpec((PEER_SEL, tt), lambda i: (0, i))
    return pl.pallas_call(
        _topk_kernel,
        grid=(n // tt,),
        in_specs=[pl.BlockSpec((2 * PEER_HEADS, PEER_N_KEYS, tt), lambda i: (0, 0, i))],
        out_specs=[spec, spec],
        out_shape=[jax.ShapeDtypeStruct((PEER_SEL, n), I32),
                   jax.ShapeDtypeStruct((PEER_SEL, n), F32)],
        compiler_params=_params(("parallel",)),
        name="topk",
    )(st)


GROWS = PEER_SEL * SLAB


def _pack_table(tab):
    n = tab.shape[0]
    b = lax.bitcast_convert_type(tab.astype(BF16), jnp.uint16).astype(U32)
    b = b.reshape(n, SLAB, 2, LANES)
    return ((b[:, :, 1] << 16) | b[:, :, 0]).reshape(n * SLAB, LANES)


def _gelu_tanh(x):
    return 0.5 * x * (1.0 + jnp.tanh(math.sqrt(2.0 / math.pi) * (x + 0.044715 * (x * x * x))))


def _gather_experts(e_ref, t, tab_ref, graw):
    for j in range(PEER_SEL):
        r = pl.multiple_of(e_ref[t, j], SLAB)
        graw[SLAB * j:SLAB * (j + 1), :] = tab_ref[pl.ds(r, SLAB), :]


TOK_GROUP = 8


def _token_groups(ntok, gather, group_begin, compute, group_end, bufs):
    gather(0, bufs[0])

    def group(i, _):
        base = pl.multiple_of(i * TOK_GROUP, TOK_GROUP)
        ctx = group_begin(base)
        acc = None
        for r in range(TOK_GROUP):
            gather(jnp.minimum(base + r + 1, ntok - 1), bufs[(r + 1) % 2])
            acc = compute(base, r, bufs[r % 2], ctx, acc)
        group_end(base, acc)
        return 0

    lax.fori_loop(0, ntok // TOK_GROUP, group, 0)


def _group_diag():
    row = lax.broadcasted_iota(I32, (8, 2 * GROWS), 0)
    lane = lax.broadcasted_iota(I32, (8, 2 * GROWS), 1)
    return row, (lane % 8) == row


def _peer_u_kernel(e_ref, h_ref, g_ref, tab_ref, gsum_ref, w_ref, graw0, graw1):
    row, diag = _group_diag()

    def compute(base, r, graw, ctx, acc):
        b = pltpu.bitcast(graw[...], BF16)
        tt = lax.dot_general(h_ref[base + r], b, NT_DIMS, preferred_element_type=F32)
        t1 = jnp.sum(jnp.where(diag, tt, 0.0), axis=0, keepdims=True)
        placed = jnp.where(row == r, t1, 0.0)
        return placed if acc is None else acc + placed

    def group_end(base, t8):
        hi = t8.astype(BF16)
        r1 = t8 - hi.astype(F32)
        mid = r1.astype(BF16)
        lo = (r1 - mid.astype(F32)).astype(BF16)
        dd = jnp.dot(jnp.concatenate([hi, mid, lo], axis=0), gsum_ref[...],
                     preferred_element_type=F32)
        d = dd[0:8] + dd[8:16] + dd[16:24]
        w = g_ref[pl.ds(base, TOK_GROUP), :] * _gelu_tanh(d)
        w_ref[pl.ds(base, TOK_GROUP), :] = w.astype(BF16).astype(F32)

    gather = lambda t, graw: _gather_experts(e_ref, t, tab_ref, graw)
    _token_groups(h_ref.shape[0], gather, lambda base: None, compute, group_end, (graw0, graw1))


def _peer_u(e4, h3, g, tab, tb):
    n = e4.shape[0]
    j = jnp.arange(2 * GROWS)
    gsum = (j[:, None] // 8 == jnp.arange(PEER_SEL)[None, :]).astype(BF16)
    tok = pl.BlockSpec((tb, PEER_SEL), lambda i: (i, 0))
    return pl.pallas_call(
        _peer_u_kernel,
        grid=(n // tb,),
        in_specs=[pl.BlockSpec((tb, PEER_SEL), lambda i: (i, 0), memory_space=pltpu.SMEM),
                  pl.BlockSpec((tb, 8, LANES), lambda i: (i, 0, 0)), tok,
                  _const_spec(tab.shape), _const_spec(gsum.shape)],
        out_specs=tok,
        out_shape=jax.ShapeDtypeStruct((n, PEER_SEL), F32),
        scratch_shapes=[pltpu.VMEM((GROWS, LANES), U32)] * 2,
        compiler_params=_params(("parallel",)),
        name="peer_u",
    )(e4, h3, g, tab, gsum)


def _peer_v_kernel(e_ref, w_ref, tab_ref, expand_ref, o_ref, graw0, graw1):
    _, diag = _group_diag()

    def group_begin(base):
        w8 = w_ref[pl.ds(base, TOK_GROUP), :].astype(BF16)
        return jnp.dot(w8, expand_ref[...], preferred_element_type=F32)

    def compute(base, r, graw, wide, acc):
        b = pltpu.bitcast(graw[...], BF16)
        lhs = jnp.where(diag, jnp.broadcast_to(wide[r:r + 1, :], diag.shape), 0.0).astype(BF16)
        o_ref[base + r] = jnp.dot(lhs, b, preferred_element_type=F32)
        return acc

    gather = lambda t, graw: _gather_experts(e_ref, t, tab_ref, graw)
    _token_groups(o_ref.shape[0], gather, group_begin, compute, lambda base, acc: None,
                  (graw0, graw1))


def _peer_v(e4, w, tab, tb):
    n = e4.shape[0]
    j = jnp.arange(2 * GROWS)
    expand = (jnp.arange(PEER_SEL)[:, None] == j[None, :] // 8).astype(BF16)
    return pl.pallas_call(
        _peer_v_kernel,
        grid=(n // tb,),
        in_specs=[pl.BlockSpec((tb, PEER_SEL), lambda i: (i, 0), memory_space=pltpu.SMEM),
                  pl.BlockSpec((tb, PEER_SEL), lambda i: (i, 0)),
                  _const_spec(tab.shape), _const_spec(expand.shape)],
        out_specs=pl.BlockSpec((tb, 8, LANES), lambda i: (i, 0, 0)),
        out_shape=jax.ShapeDtypeStruct((n, 8, LANES), F32),
        scratch_shapes=[pltpu.VMEM((GROWS, LANES), U32)] * 2,
        compiler_params=_params(("parallel",)),
        name="peer_v",
    )(e4, w, tab, expand).reshape(n, D_MODEL)


def _ple_kernel(x1_ref, peer_ref, p_ref, gple_ref, wgate_ref, wproj_ref, gfin_ref, y_ref):
    x2 = x1_ref[...] + peer_ref[...]
    hn = _rms(x2, gple_ref[...]).astype(BF16)
    gate = jax.nn.sigmoid(jnp.dot(hn, wgate_ref[...], preferred_element_type=F32))
    pp = jnp.dot(p_ref[...].astype(BF16), wproj_ref[...], preferred_element_type=F32)
    y_ref[...] = _rms(x2 + gate * pp, gfin_ref[...])


def _ple(x1, peer, p, g_ple, w_gate, w_proj, g_final, tm):
    n = x1.shape[0]
    tok = lambda w: pl.BlockSpec((tm, w), lambda i: (i, 0))
    return pl.pallas_call(
        _ple_kernel,
        grid=(n // tm,),
        in_specs=[tok(D_MODEL), tok(D_MODEL), tok(PLE_DIM), _const_spec((1, D_MODEL)),
                  _const_spec(w_gate.shape), _const_spec(w_proj.shape), _const_spec((1, D_MODEL))],
        out_specs=tok(D_MODEL),
        out_shape=jax.ShapeDtypeStruct((n, D_MODEL), F32),
        compiler_params=_params(("parallel",)),
        name="ple",
    )(x1, peer, p, g_ple.reshape(1, D_MODEL), w_gate.astype(BF16), w_proj.astype(BF16),
      g_final.reshape(1, D_MODEL))


def _channel(x, od, of, ga, gb, p, lw, u_tab, v_tab, g_final, tm, tt, tb):
    n = x.shape[0]
    x1, hb, st = _merge(x, od, of, ga, gb, lw["w_pa"], lw["w_pb"], lw["w_o"], lw["g_ffn"],
                        lw["w_pq"], lw["sub_k1"], lw["sub_k2"], tm)
    e_t, g_t = _topk(st, tt)
    e4, g = e_t.T, g_t.T
    tb = min(tb, n)
    w = _peer_u(e4, hb.reshape(n, 8, LANES), g, u_tab, tb)
    peer = _peer_v(e4, w, v_tab, tb)
    return _ple(x1, peer, p, lw["g_ple"], lw["w_ple_gate"], lw["w_ple_proj"], g_final, tm)


def kernel(x_prompt, x_sample, p_prompt, p_sample, cache_diff_k, cache_diff_v, cache_fox_k,
           cache_fox_v, cache_fox_logf, page_table, rel_table, g_attn, w_in, b_f, lam_q1, lam_k1,
           lam_q2, lam_k2, g_subln, w_pa, w_pb, w_o, g_ffn, w_pq, sub_k1, sub_k2, peer_u, peer_v,
           g_ple, w_ple_gate, w_ple_proj, g_final):
    batch, seq, _ = x_prompt.shape
    ns = x_sample.shape[0]
    depth = g_attn.shape[0]
    assert depth == 1 and x_sample.shape[1] == 1
    l = 0
    lam_init = 0.8 - 0.6 * math.exp(-0.3 * l)
    npool, page = cache_diff_k.shape[1], cache_diff_k.shape[2]
    past = page_table.shape[1] * page
    t_attn = min(512, seq)
    lw = dict(w_pa=w_pa[l], w_pb=w_pb[l], w_o=w_o[l], g_ffn=g_ffn[l], w_pq=w_pq[l],
              sub_k1=sub_k1[l], sub_k2=sub_k2[l], g_ple=g_ple[l], w_ple_gate=w_ple_gate[l],
              w_ple_proj=w_ple_proj[l])
    u_tab = _pack_table(peer_u[l])
    v_tab = _pack_table(peer_v[l])
    lams = (lam_q1[l], lam_k1[l], lam_q2[l], lam_k2[l])

    xp = x_prompt.reshape(batch * seq, D_MODEL)
    (dq, dkb, _, fq, fkb, _, dk, dv, fk, fv, lf, lft, ga, gb, dvt, fvt) = _inproj(
        xp, g_attn[l], w_in[l], b_f[l], 256)
    c = _cumsum(lft, seq).T
    bias = _bias_tiles(rel_table, t_attn)
    od = _prompt_attn(True, dq, dkb, dvt, batch, seq, t_attn, (bias, *lams, g_subln[l]), lam_init)
    of = _prompt_attn(False, fq, fkb, fvt, batch, seq, t_attn, (c,))
    yp = _channel(xp, od, of, ga, gb, p_prompt[l].reshape(batch * seq, PLE_DIM), lw, u_tab, v_tab,
                  g_final, 256, 128, 64)
    outs_p = (dk.reshape(1, batch, seq, DIFF_HEADS, 2, DIFF_HEAD_DIM),
              dv.reshape(1, batch, seq, DIFF_HEADS, DIFF_V_DIM),
              fk.reshape(1, batch, seq, FOX_HEADS, FOX_HEAD_DIM),
              fv.reshape(1, batch, seq, FOX_HEADS, FOX_HEAD_DIM),
              lf.reshape(1, batch, seq, FOX_HEADS))

    xs = x_sample.reshape(ns, D_MODEL)
    (sdq, sdkb, sdvb, sfq, sfkb, sfvb, sdk, sdv, sfk, sfv, slf, _, sga, sgb, _, _) = _inproj(
        xs, g_attn[l], w_in[l], b_f[l], ns)
    rel_page = past - (past - page + jnp.arange(page))
    bias_pg = jnp.repeat(_shifted_bias(rel_page, rel_table), 2, axis=0)
    bias_self = jnp.repeat(_shifted_bias(jnp.zeros((1,), I32), rel_table), 2, axis=0)
    feat_major = lambda c: jnp.moveaxis(c.reshape(npool, page, 512), 1, 2)
    lfp = _page_prefix(jnp.swapaxes(cache_fox_logf[l], 1, 2).reshape(npool * FOX_HEADS, page), 1024)
    sod, sof = _sample_attn(
        page_table, sdq, sdkb, sdvb, sfq, sfkb, sfvb, slf,
        feat_major(cache_diff_k[l]), cache_diff_v[l].reshape(npool, page * DIFF_HEADS, DIFF_V_DIM),
        feat_major(cache_fox_k[l]), feat_major(cache_fox_v[l]),
        lfp.reshape(npool, FOX_HEADS, page), bias_pg, bias_self, *lams, g_subln[l], lam_init)
    ys = _channel(xs, sod, sof, sga, sgb, p_sample[l].reshape(ns, PLE_DIM), lw, u_tab, v_tab,
                  g_final, ns, ns, 32)
    outs_s = (sdk.reshape(1, ns, 1, DIFF_HEADS, 2, DIFF_HEAD_DIM),
              sdv.reshape(1, ns, 1, DIFF_HEADS, DIFF_V_DIM),
              sfk.reshape(1, ns, 1, FOX_HEADS, FOX_HEAD_DIM),
              sfv.reshape(1, ns, 1, FOX_HEADS, FOX_HEAD_DIM),
              slf.reshape(1, ns, 1, FOX_HEADS))
    return (yp.reshape(batch, seq, D_MODEL), ys.reshape(ns, 1, D_MODEL)) + outs_p + outs_s
```

```python
import functools
import math

import jax
import jax.numpy as jnp
from jax import lax
from jax.experimental import pallas as pl
from jax.experimental.pallas import tpu as pltpu

F32 = jnp.float32
BF16 = jnp.bfloat16
U32 = jnp.uint32
I32 = jnp.int32

D_MODEL = 1024
DIFF_HEADS = 4
DIFF_HEAD_DIM = 64
DIFF_V_DIM = 128
DIFF_WIDTH = 512
FOX_HEADS = 8
FOX_HEAD_DIM = 64
FOX_WIDTH = 512
ATTN_SCALE = 0.125
REL_BUCKETS = 32
REL_MAX_DIST = 128
PEER_HEADS = 8
PEER_N_KEYS = 128
PEER_HALF = 128
PEER_TOPK = 16
PEER_SEL = PEER_HEADS * PEER_TOPK
PLE_DIM = 256
EPS = 1e-6
NEG_INF = -1e30
LANES = 128
SLAB = 4
VMEM_LIMIT = 56 * 1024 * 1024

NT_DIMS = (((1,), (1,)), ((), ()))


def _const_spec(shape):
    nd = len(shape)
    return pl.BlockSpec(shape, lambda *_: (0,) * nd, pipeline_mode=pl.Buffered(1))


def _params(sem, vmem=VMEM_LIMIT):
    return pltpu.CompilerParams(dimension_semantics=sem, vmem_limit_bytes=vmem)


def _rms(x, g):
    return x * lax.rsqrt(jnp.mean(x * x, axis=-1, keepdims=True) + EPS) * g


def _log_sigmoid(x):
    return -(jnp.maximum(-x, 0.0) + jnp.log1p(jnp.exp(-jnp.abs(x))))


def _inproj_kernel(x_ref, g_ref, wqkv_ref, wfl_ref, wflt_ref, wg_ref, bfr_ref, bfc_ref, wvt_ref,
                   dq_ref, dkb_ref, dvb_ref, fq_ref, fkb_ref, fvb_ref,
                   dk_ref, dv_ref, fk_ref, fv_ref, lf_ref, lft_ref, ga_ref, gb_ref,
                   dvt_ref, fvt_ref):
    h = _rms(x_ref[...], g_ref[...]).astype(BF16)
    dvt_ref[...] = lax.dot_general(wvt_ref[:512, :], h, NT_DIMS,
                                   preferred_element_type=F32).astype(BF16)
    fvt_ref[...] = lax.dot_general(wvt_ref[512:, :], h, NT_DIMS,
                                   preferred_element_type=F32).astype(BF16)

    def proj(c):
        return jnp.dot(h, wqkv_ref[:, c * 512:(c + 1) * 512], preferred_element_type=F32)

    dq_ref[...] = (proj(0) * ATTN_SCALE).astype(BF16)
    z = proj(1); dk_ref[...] = z; dkb_ref[...] = z.astype(BF16)
    z = proj(2); dv_ref[...] = z; dvb_ref[...] = z.astype(BF16)
    fq_ref[...] = (proj(3) * ATTN_SCALE).astype(BF16)
    z = proj(4); fk_ref[...] = z; fkb_ref[...] = z.astype(BF16)
    z = proj(5); fv_ref[...] = z; fvb_ref[...] = z.astype(BF16)

    fl = jnp.dot(h, wfl_ref[...], preferred_element_type=F32)
    lf_ref[...] = _log_sigmoid(fl[:, :FOX_HEADS] + bfr_ref[...])
    flt = lax.dot_general(wflt_ref[...], h, NT_DIMS, preferred_element_type=F32)
    lft_ref[...] = _log_sigmoid(flt + bfc_ref[...])

    ga_ref[...] = jax.nn.sigmoid(jnp.dot(h, wg_ref[:, :D_MODEL], preferred_element_type=F32))
    gb_ref[...] = jax.nn.sigmoid(jnp.dot(h, wg_ref[:, D_MODEL:], preferred_element_type=F32))


def _inproj(x, g_attn, w_in, b_f, tm):
    n = x.shape[0]
    wqkv = w_in[:, :3072].astype(BF16)
    wfl = jnp.pad(w_in[:, 3072:3080], ((0, 0), (0, LANES - FOX_HEADS))).astype(BF16)
    wflt = w_in[:, 3072:3080].T.astype(BF16)
    wg = w_in[:, 3080:].astype(BF16)
    wvt = jnp.concatenate([w_in[:, 1024:1536], w_in[:, 2560:3072]], axis=1).T.astype(BF16)
    tok = lambda w: pl.BlockSpec((tm, w), lambda i: (i, 0))
    feat = pl.BlockSpec((512, tm), lambda i: (0, i))
    sds = jax.ShapeDtypeStruct
    out_shape = ([sds((n, 512), BF16)] * 6 + [sds((n, 512), F32)] * 4
                 + [sds((n, FOX_HEADS), F32), sds((FOX_HEADS, n), F32)]
                 + [sds((n, D_MODEL), F32)] * 2 + [sds((512, n), BF16)] * 2)
    out_specs = ([tok(512)] * 10
                 + [tok(FOX_HEADS), pl.BlockSpec((FOX_HEADS, tm), lambda i: (0, i))]
                 + [tok(D_MODEL)] * 2 + [feat, feat])
    return pl.pallas_call(
        _inproj_kernel,
        grid=(n // tm,),
        in_specs=[tok(D_MODEL), _const_spec((1, D_MODEL)), _const_spec(wqkv.shape),
                  _const_spec(wfl.shape), _const_spec(wflt.shape), _const_spec(wg.shape),
                  _const_spec((1, FOX_HEADS)), _const_spec((FOX_HEADS, 1)), _const_spec(wvt.shape)],
        out_specs=out_specs,
        out_shape=out_shape,
        compiler_params=_params(("parallel",)),
        name="inproj",
    )(x, g_attn.reshape(1, D_MODEL), wqkv, wfl, wflt, wg,
      b_f.reshape(1, FOX_HEADS), b_f.reshape(FOX_HEADS, 1), wvt)


def _lane_prefix(x):
    lane = lax.broadcasted_iota(I32, x.shape, 1)
    sh = 1
    while sh < LANES:
        x = x + jnp.where(lane >= sh, pltpu.roll(x, sh, 1), 0.0)
        sh *= 2
    return x


def _cumsum_kernel(x_ref, o_ref):
    nchunk = x_ref.shape[1] // LANES
    local = [_lane_prefix(x_ref[:, c * LANES:(c + 1) * LANES]) for c in range(nchunk)]
    carry = jnp.zeros((x_ref.shape[0], 1), F32)
    for c in range(nchunk):
        o_ref[:, c * LANES:(c + 1) * LANES] = local[c] + carry
        carry = carry + local[c][:, LANES - 1:LANES]


def _cumsum(lft, seq):
    h, n = lft.shape
    return pl.pallas_call(
        _cumsum_kernel,
        grid=(n // seq,),
        in_specs=[pl.BlockSpec((h, seq), lambda b: (0, b))],
        out_specs=pl.BlockSpec((h, seq), lambda b: (0, b)),
        out_shape=jax.ShapeDtypeStruct((h, n), F32),
        compiler_params=_params(("parallel",)),
        name="cumsum",
    )(lft)


def _t5_bucket(rel):
    n = jnp.maximum(rel, 0)
    max_exact = REL_BUCKETS // 2
    nf = jnp.maximum(n, 1).astype(F32)
    large = max_exact + (jnp.log(nf / max_exact) / math.log(REL_MAX_DIST / max_exact)
                         * (REL_BUCKETS - max_exact)).astype(I32)
    large = jnp.minimum(large, REL_BUCKETS - 1)
    return jnp.where(n < max_exact, n, large)


def _far_bias(rel_table):
    return rel_table[_t5_bucket(jnp.array(REL_MAX_DIST, I32))].astype(F32)


def _shifted_bias(rel, rel_table):
    b = rel_table[_t5_bucket(rel)].astype(F32)
    return jnp.moveaxis(b - _far_bias(rel_table), -1, 0)


def _bias_tiles(rel_table, t):
    by_dist = _shifted_bias(jnp.arange(2 * t), rel_table)
    masked = jnp.full((by_dist.shape[0], t - 1), NEG_INF, F32)
    ext = jnp.concatenate([masked, by_dist[:, :t]], axis=1)

    def toeplitz(vec, off):
        length = vec.shape[1]
        period = jnp.pad(vec, ((0, 0), (0, 1)))
        rows = jnp.tile(period, (1, t))[:, :t * length].reshape(-1, t, length)
        return rows[:, :, off:off + t]

    diag = toeplitz(ext, t - 1)
    off1 = toeplitz(by_dist, t)
    return jnp.stack([diag, off1], axis=1)


def _softmax_step(s, vt, m, l, acc):
    m_new = jnp.maximum(m, jnp.max(s, axis=0, keepdims=True))
    alpha = jnp.exp(m - m_new)
    p = jnp.exp(s - m_new)
    l = alpha * l + jnp.sum(p, axis=0, keepdims=True)
    acc = alpha * acc + jnp.dot(vt, p.astype(BF16), preferred_element_type=F32)
    return m_new, l, acc


def _attn_kernel(diff, t, lam_init, *refs):
    if diff:
        q_ref, k_ref, vt_ref, bias_ref, lq1, lk1, lq2, lk2, gs_ref, o_ref = refs
    else:
        q_ref, k_ref, vt_ref, c_ref, o_ref = refs
    i = pl.program_id(2)
    lane = lax.broadcasted_iota(I32, (t, LANES), 1)
    q = q_ref[...].astype(F32)
    qa = jnp.where(lane < 64, q, 0.0).astype(BF16)
    qb = jnp.where(lane >= 64, q, 0.0).astype(BF16)

    def step(kb, carry, bias_a, bias_b):
        off = pl.multiple_of(kb * t, t)
        k = k_ref[pl.ds(off, t), :]
        vt = vt_ref[:, pl.ds(off, t)]
        sa = lax.dot_general(k, qa, NT_DIMS, preferred_element_type=F32)
        sb = lax.dot_general(k, qb, NT_DIMS, preferred_element_type=F32)
        if not diff:
            c = c_ref[pl.ds(off, t), :]
            sa = sa - c[:, 0:1]
            sb = sb - c[:, 1:2]
        if bias_a is not None:
            sa = sa + bias_a
            sb = sb + bias_b
        ma, la, acca, mb, lb, accb = carry
        ma, la, acca = _softmax_step(sa, vt, ma, la, acca)
        mb, lb, accb = _softmax_step(sb, vt, mb, lb, accb)
        return ma, la, acca, mb, lb, accb

    row0 = jnp.full((1, t), NEG_INF, F32)
    zero1 = jnp.zeros((1, t), F32)
    zacc = jnp.zeros((LANES, t), F32)
    carry = (row0, zero1, zacc, row0, zero1, zacc)

    if diff:
        n_plain = jnp.maximum(i - 1, 0)
    else:
        n_plain = i
    carry = lax.fori_loop(0, n_plain, lambda kb, c: step(kb, c, None, None), carry)
    if diff:
        off1 = bias_ref[1]
        carry = lax.cond(i >= 1, lambda c: step(i - 1, c, off1, off1), lambda c: c, carry)
        diag = bias_ref[0]
    else:
        key = lax.broadcasted_iota(I32, (t, t), 0)
        qry = lax.broadcasted_iota(I32, (t, t), 1)
        diag = jnp.where(qry >= key, 0.0, NEG_INF)
    ma, la, acca, mb, lb, accb = step(i, carry, diag, diag)

    if diff:
        lam = (jnp.exp(jnp.sum(lq1[...] * lk1[...], axis=-1, keepdims=True))
               - jnp.exp(jnp.sum(lq2[...] * lk2[...], axis=-1, keepdims=True)) + lam_init)
        o = acca / la - lam * (accb / lb)
        y = o * lax.rsqrt(jnp.mean(o * o, axis=0, keepdims=True) + EPS) * gs_ref[...]
        o_ref[...] = (y * (1.0 - lam_init)).T.astype(o_ref.dtype)
    else:
        feat = lax.broadcasted_iota(I32, (LANES, t), 0)
        o_ref[...] = jnp.where(feat < 64, acca / la, accb / lb).T.astype(o_ref.dtype)


def _prompt_attn(diff, q, k, vt, batch, seq, t, extra, lam_init=0.0):
    nslab = 512 // LANES
    nq = seq // t
    qspec = pl.BlockSpec((t, LANES), lambda b, j, i: (b * nq + i, j))
    kspec = pl.BlockSpec((seq, LANES), lambda b, j, i: (b, j))
    vtspec = pl.BlockSpec((LANES, seq), lambda b, j, i: (j, b))
    if diff:
        bias, lq1, lk1, lq2, lk2, gs = extra
        row = _const_spec((1, DIFF_HEAD_DIM))
        especs = [pl.BlockSpec((None, 2, t, t), lambda b, j, i: (j, 0, 0, 0)),
                  row, row, row, row, _const_spec((DIFF_V_DIM, 1))]
        eargs = [bias, lq1.reshape(1, -1), lk1.reshape(1, -1), lq2.reshape(1, -1),
                 lk2.reshape(1, -1), gs.reshape(-1, 1)]
    else:
        (c,) = extra
        especs = [pl.BlockSpec((None, seq, 2), lambda b, j, i: (j, b, 0))]
        eargs = [jnp.moveaxis(c.reshape(batch * seq, nslab, 2), 1, 0)]
    return pl.pallas_call(
        functools.partial(_attn_kernel, diff, t, lam_init),
        grid=(batch, nslab, nq),
        in_specs=[qspec, kspec, vtspec] + especs,
        out_specs=qspec,
        out_shape=jax.ShapeDtypeStruct((batch * seq, 512), BF16),
        compiler_params=_params(("parallel", "parallel", "arbitrary")),
        name="attn_diff" if diff else "attn_fox",
    )(q, k, vt, *eargs)


PAGES_PER_STEP = 8


def _page_prefix_kernel(x_ref, o_ref):
    x = x_ref[...]
    lane = lax.broadcasted_iota(I32, x.shape, 1)
    sh = 1
    while sh < x.shape[1]:
        x = x + jnp.where(lane >= sh, pltpu.roll(x, sh, 1), 0.0)
        sh *= 2
    o_ref[...] = x


def _page_prefix(lft_rows, rows_per_step):
    n, page = lft_rows.shape
    rows_per_step = math.gcd(n, rows_per_step)
    spec = pl.BlockSpec((rows_per_step, page), lambda i: (i, 0))
    return pl.pallas_call(
        _page_prefix_kernel,
        grid=(n // rows_per_step,),
        in_specs=[spec], out_specs=spec,
        out_shape=jax.ShapeDtypeStruct((n, page), F32),
        compiler_params=_params(("parallel",)),
        name="page_prefix",
    )(lft_rows)


def _paged_kernel(g, nsteps, page, lam_init, pt_ref, *refs):
    del pt_ref
    (qd_ref, kdn_ref, vdn_ref, qf_ref, kfn_ref, vfn_ref, lfn_ref) = refs[:7]
    kdt = refs[7:7 + g]
    vd = refs[7 + g:7 + 2 * g]
    kft = refs[7 + 2 * g:7 + 3 * g]
    vft = refs[7 + 3 * g:7 + 4 * g]
    lfp = refs[7 + 4 * g:7 + 5 * g]
    (bias_ref, bself_ref, lq1, lk1, lq2, lk2, gs_ref, od_ref, of_ref,
     md, ld, accd, mf, lf, accf, csum) = refs[7 + 5 * g:]
    p = pl.program_id(1)
    row = lax.broadcasted_iota(I32, (8, 512), 0)
    lane = lax.broadcasted_iota(I32, (8, 512), 1)
    own = (lane // 64) == row
    qd = jnp.where(own, qd_ref[0].astype(F32), 0.0)
    qf = jnp.where(own, qf_ref[0].astype(F32), 0.0)
    row1 = lax.broadcasted_iota(I32, (8, LANES), 0)

    @pl.when(p == 0)
    def _():
        for m_ref in (md, mf):
            m_ref[...] = jnp.full(m_ref.shape, NEG_INF, F32)
        for z_ref in (ld, lf, accd, accf, csum):
            z_ref[...] = jnp.zeros(z_ref.shape, F32)

    qd_b, qf_b = qd.astype(BF16), qf.astype(BF16)
    sd, sf = [], []
    carry = csum[...]
    for i in range(g):
        s = jnp.dot(qd_b, kdt[i][...].astype(BF16), preferred_element_type=F32)
        if i == g - 1:
            s = s + jnp.where(p == nsteps - 1, bias_ref[...], 0.0)
        sd.append(s)
        local = lfp[i][...]
        sf.append(jnp.dot(qf_b, kft[i][...].astype(BF16), preferred_element_type=F32)
                  - (local + carry))
        carry = carry + local[:, page - 1:page]
    csum[...] = carry
    sd = jnp.concatenate(sd, axis=1)
    sf = jnp.concatenate(sf, axis=1)

    def probs(s, m_ref, l_ref):
        m_new = jnp.maximum(m_ref[...], jnp.max(s, axis=-1, keepdims=True))
        alpha = jnp.exp(m_ref[...] - m_new)
        pr = jnp.exp(s - m_new)
        l_ref[...] = alpha * l_ref[...] + jnp.sum(pr, axis=-1, keepdims=True)
        m_ref[...] = m_new
        return alpha, pr.astype(BF16)

    alpha, pr = probs(sd, md, ld)
    acc = alpha * accd[...]
    for i in range(g):
        pi = pr[:, i * page:(i + 1) * page]
        for h in range(DIFF_HEADS):
            v_h = vd[i][pl.ds(h, page, stride=DIFF_HEADS), :].astype(BF16)
            r_h = jnp.dot(pi, v_h, preferred_element_type=F32)
            acc = acc + jnp.where(row1 // 2 == h, r_h, 0.0)
    accd[...] = acc

    alpha, pr = probs(sf, mf, lf)
    acc = alpha * accf[...]
    for i in range(g):
        acc = acc + lax.dot_general(pr[:, i * page:(i + 1) * page], vft[i][...].astype(BF16),
                                    NT_DIMS, preferred_element_type=F32)
    accf[...] = acc

    @pl.when(p == nsteps - 1)
    def _():
        def self_update(s, v_new, m_ref, l_ref, acc_ref):
            m_new = jnp.maximum(m_ref[...], s)
            alpha = jnp.exp(m_ref[...] - m_new)
            pr = jnp.exp(s - m_new)
            l = alpha * l_ref[...] + pr
            return (alpha * acc_ref[...] + pr * v_new) / l

        s_self = jnp.sum(qd * kdn_ref[0].astype(F32), axis=-1, keepdims=True) + bself_ref[...]
        vn = vdn_ref[0].astype(F32)
        vn8 = jnp.concatenate([vn[:, (r // 2) * DIFF_V_DIM:(r // 2 + 1) * DIFF_V_DIM]
                               for r in range(8)], axis=0)
        rd = self_update(s_self, vn8, md, ld, accd)
        s_self = (jnp.sum(qf * kfn_ref[0].astype(F32), axis=-1, keepdims=True)
                  - (csum[...] + lfn_ref[0]))
        rf = self_update(s_self, vfn_ref[0].astype(F32), mf, lf, accf)

        lam = (jnp.exp(jnp.sum(lq1[...] * lk1[...], axis=-1, keepdims=True))
               - jnp.exp(jnp.sum(lq2[...] * lk2[...], axis=-1, keepdims=True)) + lam_init)
        parts = []
        for h in range(DIFF_HEADS):
            seg = rd[2 * h:2 * h + 1, :] - lam * rd[2 * h + 1:2 * h + 2, :]
            parts.append(_rms(seg, gs_ref[...]) * (1.0 - lam_init))
        od_ref[0] = jnp.concatenate(parts, axis=-1).astype(od_ref.dtype)
        of_ref[0] = jnp.sum(jnp.where(own, rf, 0.0), axis=0, keepdims=True).astype(of_ref.dtype)


def _sample_attn(page_table, qd, kdn, vdn, qf, kfn, vfn, lfn, kdt, vd, kft, vft, lfp,
                 bias, bself, lq1, lk1, lq2, lk2, gs, lam_init):
    ns, npages = page_table.shape
    page = kdt.shape[2]
    g = math.gcd(npages, PAGES_PER_STEP)
    nsteps = npages // g
    tok = pl.BlockSpec((1, 1, 512), lambda b, p, pt: (b, 0, 0))
    const = lambda shape: pl.BlockSpec(shape, lambda b, p, pt: (0,) * len(shape))
    row = const((1, DIFF_HEAD_DIM))
    r3 = lambda a: a.reshape(ns, 1, 512)

    def pages(rows):
        return [pl.BlockSpec((None, rows, page), lambda b, p, pt, i=i: (pt[b, p * g + i], 0, 0))
                for i in range(g)]

    grid_spec = pltpu.PrefetchScalarGridSpec(
        num_scalar_prefetch=1,
        grid=(ns, nsteps),
        in_specs=([tok] * 6 + [pl.BlockSpec((1, FOX_HEADS, 1), lambda b, p, pt: (b, 0, 0))]
                  + pages(512) * 4 + pages(FOX_HEADS)
                  + [const((8, page)), const((8, 1)), row, row, row, row, const((1, DIFF_V_DIM))]),
        out_specs=[tok, tok],
        scratch_shapes=[pltpu.VMEM((8, 1), F32), pltpu.VMEM((8, 1), F32), pltpu.VMEM((8, LANES), F32),
                        pltpu.VMEM((8, 1), F32), pltpu.VMEM((8, 1), F32), pltpu.VMEM((8, 512), F32),
                        pltpu.VMEM((8, 1), F32)],
    )
    od, of = pl.pallas_call(
        functools.partial(_paged_kernel, g, nsteps, page, lam_init),
        grid_spec=grid_spec,
        out_shape=[jax.ShapeDtypeStruct((ns, 1, 512), BF16)] * 2,
        compiler_params=_params(("parallel", "arbitrary")),
        name="attn_paged",
    )(page_table, r3(qd), r3(kdn), r3(vdn), r3(qf), r3(kfn), r3(vfn),
      lfn.reshape(ns, FOX_HEADS, 1), *([kdt] * g), *([vd] * g), *([kft] * g), *([vft] * g),
      *([lfp] * g), bias, bself,
      lq1.reshape(1, -1), lk1.reshape(1, -1), lq2.reshape(1, -1), lk2.reshape(1, -1),
      gs.reshape(1, -1))
    return od.reshape(ns, 512), of.reshape(ns, 512)


def _merge_kernel(x_ref, od_ref, of_ref, ga_ref, gb_ref, wpa_ref, wpb_ref, wo_ref,
                  gffn_ref, wpq_ref, k1_ref, k2_ref, x1_ref, hb_ref, st_ref):
    ya = jnp.dot(od_ref[...], wpa_ref[...], preferred_element_type=F32)
    yb = jnp.dot(of_ref[...], wpb_ref[...], preferred_element_type=F32)
    t = (ga_ref[...] * ya + gb_ref[...] * yb).astype(BF16)
    x1 = x_ref[...] + jnp.dot(t, wo_ref[...], preferred_element_type=F32)
    x1_ref[...] = x1
    hb = _rms(x1, gffn_ref[...]).astype(BF16)
    hb_ref[...] = hb
    for hm in range(2 * PEER_HEADS):
        q = jnp.dot(hb, wpq_ref[:, hm * PEER_HALF:(hm + 1) * PEER_HALF],
                    preferred_element_type=F32).astype(BF16)
        sub = k1_ref if hm % 2 == 0 else k2_ref
        st_ref[hm] = lax.dot_general(sub[...], q, NT_DIMS, preferred_element_type=F32)


def _merge(x, od, of, ga, gb, w_pa, w_pb, w_o, g_ffn, w_pq, sub_k1, sub_k2, tm):
    n = x.shape[0]
    tok = lambda w: pl.BlockSpec((tm, w), lambda i: (i, 0))
    bf = lambda a: a.astype(BF16)
    sds = jax.ShapeDtypeStruct
    return pl.pallas_call(
        _merge_kernel,
        grid=(n // tm,),
        in_specs=[tok(D_MODEL), tok(512), tok(512), tok(D_MODEL), tok(D_MODEL),
                  _const_spec(w_pa.shape), _const_spec(w_pb.shape), _const_spec(w_o.shape),
                  _const_spec((1, D_MODEL)), _const_spec(w_pq.shape),
                  _const_spec(sub_k1.shape), _const_spec(sub_k2.shape)],
        out_specs=[tok(D_MODEL), tok(D_MODEL),
                   pl.BlockSpec((2 * PEER_HEADS, PEER_N_KEYS, tm), lambda i: (0, 0, i))],
        out_shape=[sds((n, D_MODEL), F32), sds((n, D_MODEL), BF16),
                   sds((2 * PEER_HEADS, PEER_N_KEYS, n), F32)],
        compiler_params=_params(("parallel",)),
        name="merge",
    )(x, od, of, ga, gb, bf(w_pa), bf(w_pb), bf(w_o), g_ffn.reshape(1, D_MODEL), bf(w_pq),
      bf(sub_k1), bf(sub_k2))


def _top16(x):
    rows = lax.broadcasted_iota(I32, x.shape, 0)
    vals, idxs = [], []
    for _ in range(PEER_TOPK):
        m = jnp.max(x, axis=0, keepdims=True)
        idx = jnp.min(jnp.where(x == m, rows, x.shape[0]), axis=0, keepdims=True)
        x = jnp.where(rows == idx, -jnp.inf, x)
        vals.append(m)
        idxs.append(idx)
    return jnp.concatenate(vals, axis=0), jnp.concatenate(idxs, axis=0)


def _top16_of_sums(a, b):
    sub = lax.broadcasted_iota(I32, (8, a.shape[1]), 0)
    pieces = [a[0:1, :] + b]
    for i in range(1, 8):
        piece = a[i:i + 1, :] + b[0:8, :]
        live = PEER_TOPK // (i + 1)
        pieces.append(piece if live >= 8 else jnp.where(sub < live, piece, -jnp.inf))
    pieces.append(a[8:16, :] + b[0:1, :])
    sc, ridx = _top16(jnp.concatenate(pieces, axis=0))
    blk, low = ridx >> 3, ridx & 7
    i_sel = jnp.where(blk <= 1, 0, jnp.where(blk == 9, 8 + low, blk - 1))
    j_sel = jnp.where(blk <= 1, ridx, jnp.where(blk == 9, 0, low))
    return sc, i_sel, j_sel


def _select_rows(table, sel):
    out = jnp.zeros_like(table)
    for i in range(PEER_TOPK):
        out = out + jnp.where(sel == i, table[i:i + 1, :], 0)
    return out


def _topk_kernel(st_ref, e_ref, g_ref):
    def head(h, _):
        a, ia = _top16(st_ref[2 * h])
        b, ib = _top16(st_ref[2 * h + 1])
        sc, i_sel, j_sel = _top16_of_sums(a, b)
        e = _select_rows(ia, i_sel) * PEER_N_KEYS + _select_rows(ib, j_sel)
        ex = jnp.exp(sc - sc[0:1, :])
        g = ex / jnp.sum(ex, axis=0, keepdims=True)
        off = pl.multiple_of(h * PEER_TOPK, PEER_TOPK)
        e_ref[pl.ds(off, PEER_TOPK), :] = e * SLAB
        g_ref[pl.ds(off, PEER_TOPK), :] = g
        return 0

    lax.fori_loop(0, PEER_HEADS, head, 0)


def _topk(st, tt):
    n = st.shape[2]
    spec = pl.BlockSpec((PEER_SEL, tt), lambda i: (0, i))
    return pl.pallas_call(
        _topk_kernel,
        grid=(n // tt,),
        in_specs=[pl.BlockSpec((2 * PEER_HEADS, PEER_N_KEYS, tt), lambda i: (0, 0, i))],
        out_specs=[spec, spec],
        out_shape=[jax.ShapeDtypeStruct((PEER_SEL, n), I32),
                   jax.ShapeDtypeStruct((PEER_SEL, n), F32)],
        compiler_params=_params(("parallel",)),
        name="topk",
    )(st)


GROWS = PEER_SEL * SLAB


def _pack_table(tab):
    n = tab.shape[0]
    b = lax.bitcast_convert_type(tab.astype(BF16), jnp.uint16).astype(U32)
    b = b.reshape(n, SLAB, 2, LANES)
    return ((b[:, :, 1] << 16) | b[:, :, 0]).reshape(n * SLAB, LANES)


def _gelu_tanh(x):
    return 0.5 * x * (1.0 + jnp.tanh(math.sqrt(2.0 / math.pi) * (x + 0.044715 * (x * x * x))))


def _gather_experts(e_ref, t, tab_ref, graw):
    for j in range(PEER_SEL):
        r = pl.multiple_of(e_ref[t, j], SLAB)
        graw[SLAB * j:SLAB * (j + 1), :] = tab_ref[pl.ds(r, SLAB), :]


TOK_GROUP = 8


def _token_groups(ntok, gather, group_begin, compute, group_end, bufs):
    gather(0, bufs[0])

    def group(i, _):
        base = pl.multiple_of(i * TOK_GROUP, TOK_GROUP)
        ctx = group_begin(base)
        acc = None
        for r in range(TOK_GROUP):
            gather(jnp.minimum(base + r + 1, ntok - 1), bufs[(r + 1) % 2])
            acc = compute(base, r, bufs[r % 2], ctx, acc)
        group_end(base, acc)
        return 0

    lax.fori_loop(0, ntok // TOK_GROUP, group, 0)


_PEER_SCRATCH = [pltpu.VMEM((GROWS, LANES), U32)] * 2


def _group_diag():
    row = lax.broadcasted_iota(I32, (8, 2 * GROWS), 0)
    lane = lax.broadcasted_iota(I32, (8, 2 * GROWS), 1)
    return row, (lane % 8) == row


def _peer_u_kernel(e_ref, h_ref, g_ref, tab_ref, gsum_ref, w_ref, graw0, graw1):
    row, diag = _group_diag()

    def compute(base, r, graw, ctx, acc):
        b = pltpu.bitcast(graw[...], BF16)
        tt = lax.dot_general(h_ref[base + r], b, NT_DIMS, preferred_element_type=F32)
        t1 = jnp.sum(jnp.where(diag, tt, 0.0), axis=0, keepdims=True)
        placed = jnp.where(row == r, t1, 0.0)
        return placed if acc is None else acc + placed

    def group_end(base, t8):
        hi = t8.astype(BF16)
        r1 = t8 - hi.astype(F32)
        mid = r1.astype(BF16)
        lo = (r1 - mid.astype(F32)).astype(BF16)
        dd = jnp.dot(jnp.concatenate([hi, mid, lo], axis=0), gsum_ref[...],
                     preferred_element_type=F32)
        d = dd[0:8] + dd[8:16] + dd[16:24]
        w = g_ref[pl.ds(base, TOK_GROUP), :] * _gelu_tanh(d)
        w_ref[pl.ds(base, TOK_GROUP), :] = w.astype(BF16).astype(F32)

    gather = lambda t, graw: _gather_experts(e_ref, t, tab_ref, graw)
    _token_groups(h_ref.shape[0], gather, lambda base: None, compute, group_end, (graw0, graw1))


def _peer_u(e4, h3, g, tab, tb):
    n = e4.shape[0]
    j = jnp.arange(2 * GROWS)
    gsum = (j[:, None] // 8 == jnp.arange(PEER_SEL)[None, :]).astype(BF16)
    tok = pl.BlockSpec((tb, PEER_SEL), lambda i: (i, 0))
    return pl.pallas_call(
        _peer_u_kernel,
        grid=(n // tb,),
        in_specs=[pl.BlockSpec((tb, PEER_SEL), lambda i: (i, 0), memory_space=pltpu.SMEM),
                  pl.BlockSpec((tb, 8, LANES), lambda i: (i, 0, 0)), tok,
                  _const_spec(tab.shape), _const_spec(gsum.shape)],
        out_specs=tok,
        out_shape=jax.ShapeDtypeStruct((n, PEER_SEL), F32),
        scratch_shapes=_PEER_SCRATCH,
        compiler_params=_params(("parallel",)),
        name="peer_u",
    )(e4, h3, g, tab, gsum)


def _peer_v_kernel(e_ref, w_ref, tab_ref, expand_ref, o_ref, graw0, graw1):
    _, diag = _group_diag()

    def group_begin(base):
        w8 = w_ref[pl.ds(base, TOK_GROUP), :].astype(BF16)
        return jnp.dot(w8, expand_ref[...], preferred_element_type=F32)

    def compute(base, r, graw, wide, acc):
        b = pltpu.bitcast(graw[...], BF16)
        lhs = jnp.where(diag, jnp.broadcast_to(wide[r:r + 1, :], diag.shape), 0.0).astype(BF16)
        o_ref[base + r] = jnp.dot(lhs, b, preferred_element_type=F32)
        return acc

    gather = lambda t, graw: _gather_experts(e_ref, t, tab_ref, graw)
    _token_groups(o_ref.shape[0], gather, group_begin, compute, lambda base, acc: None,
                  (graw0, graw1))


def _peer_v(e4, w, tab, tb):
    n = e4.shape[0]
    j = jnp.arange(2 * GROWS)
    expand = (jnp.arange(PEER_SEL)[:, None] == j[None, :] // 8).astype(BF16)
    return pl.pallas_call(
        _peer_v_kernel,
        grid=(n // tb,),
        in_specs=[pl.BlockSpec((tb, PEER_SEL), lambda i: (i, 0), memory_space=pltpu.SMEM),
                  pl.BlockSpec((tb, PEER_SEL), lambda i: (i, 0)),
                  _const_spec(tab.shape), _const_spec(expand.shape)],
        out_specs=pl.BlockSpec((tb, 8, LANES), lambda i: (i, 0, 0)),
        out_shape=jax.ShapeDtypeStruct((n, 8, LANES), F32),
        scratch_shapes=_PEER_SCRATCH,
        compiler_params=_params(("parallel",)),
        name="peer_v",
    )(e4, w, tab, expand).reshape(n, D_MODEL)


def _ple_kernel(x1_ref, peer_ref, p_ref, gple_ref, wgate_ref, wproj_ref, gfin_ref, y_ref):
    x2 = x1_ref[...] + peer_ref[...]
    hn = _rms(x2, gple_ref[...]).astype(BF16)
    gate = jax.nn.sigmoid(jnp.dot(hn, wgate_ref[...], preferred_element_type=F32))
    pp = jnp.dot(p_ref[...].astype(BF16), wproj_ref[...], preferred_element_type=F32)
    y_ref[...] = _rms(x2 + gate * pp, gfin_ref[...])


def _ple(x1, peer, p, g_ple, w_gate, w_proj, g_final, tm):
    n = x1.shape[0]
    tok = lambda w: pl.BlockSpec((tm, w), lambda i: (i, 0))
    return pl.pallas_call(
        _ple_kernel,
        grid=(n // tm,),
        in_specs=[tok(D_MODEL), tok(D_MODEL), tok(PLE_DIM), _const_spec((1, D_MODEL)),
                  _const_spec(w_gate.shape), _const_spec(w_proj.shape), _const_spec((1, D_MODEL))],
        out_specs=tok(D_MODEL),
        out_shape=jax.ShapeDtypeStruct((n, D_MODEL), F32),
        compiler_params=_params(("parallel",)),
        name="ple",
    )(x1, peer, p, g_ple.reshape(1, D_MODEL), w_gate.astype(BF16), w_proj.astype(BF16),
      g_final.reshape(1, D_MODEL))


def _channel(x, od, of, ga, gb, p, lw, u_tab, v_tab, g_final, tm, tt, tb):
    n = x.shape[0]
    x1, hb, st = _merge(x, od, of, ga, gb, lw["w_pa"], lw["w_pb"], lw["w_o"], lw["g_ffn"],
                        lw["w_pq"], lw["sub_k1"], lw["sub_k2"], tm)
    e_t, g_t = _topk(st, tt)
    e4, g = e_t.T, g_t.T
    tb = min(tb, n)
    w = _peer_u(e4, hb.reshape(n, 8, LANES), g, u_tab, tb)
    peer = _peer_v(e4, w, v_tab, tb)
    return _ple(x1, peer, p, lw["g_ple"], lw["w_ple_gate"], lw["w_ple_proj"], g_final, tm)


def kernel(x_prompt, x_sample, p_prompt, p_sample, cache_diff_k, cache_diff_v, cache_fox_k,
           cache_fox_v, cache_fox_logf, page_table, rel_table, g_attn, w_in, b_f, lam_q1, lam_k1,
           lam_q2, lam_k2, g_subln, w_pa, w_pb, w_o, g_ffn, w_pq, sub_k1, sub_k2, peer_u, peer_v,
           g_ple, w_ple_gate, w_ple_proj, g_final):
    batch, seq, _ = x_prompt.shape
    ns = x_sample.shape[0]
    depth = g_attn.shape[0]
    assert depth == 1 and x_sample.shape[1] == 1
    l = 0
    lam_init = 0.8 - 0.6 * math.exp(-0.3 * l)
    npool, page = cache_diff_k.shape[1], cache_diff_k.shape[2]
    past = page_table.shape[1] * page
    t_attn = min(512, seq)
    lw = dict(w_pa=w_pa[l], w_pb=w_pb[l], w_o=w_o[l], g_ffn=g_ffn[l], w_pq=w_pq[l],
              sub_k1=sub_k1[l], sub_k2=sub_k2[l], g_ple=g_ple[l], w_ple_gate=w_ple_gate[l],
              w_ple_proj=w_ple_proj[l])
    u_tab = _pack_table(peer_u[l])
    v_tab = _pack_table(peer_v[l])
    lams = (lam_q1[l], lam_k1[l], lam_q2[l], lam_k2[l])

    xp = x_prompt.reshape(batch * seq, D_MODEL)
    (dq, dkb, _, fq, fkb, _, dk, dv, fk, fv, lf, lft, ga, gb, dvt, fvt) = _inproj(
        xp, g_attn[l], w_in[l], b_f[l], 256)
    c = _cumsum(lft, seq).T
    bias = _bias_tiles(rel_table, t_attn)
    od = _prompt_attn(True, dq, dkb, dvt, batch, seq, t_attn, (bias, *lams, g_subln[l]), lam_init)
    of = _prompt_attn(False, fq, fkb, fvt, batch, seq, t_attn, (c,))
    yp = _channel(xp, od, of, ga, gb, p_prompt[l].reshape(batch * seq, PLE_DIM), lw, u_tab, v_tab,
                  g_final, 256, 128, 128)
    outs_p = (dk.reshape(1, batch, seq, DIFF_HEADS, 2, DIFF_HEAD_DIM),
              dv.reshape(1, batch, seq, DIFF_HEADS, DIFF_V_DIM),
              fk.reshape(1, batch, seq, FOX_HEADS, FOX_HEAD_DIM),
              fv.reshape(1, batch, seq, FOX_HEADS, FOX_HEAD_DIM),
              lf.reshape(1, batch, seq, FOX_HEADS))

    xs = x_sample.reshape(ns, D_MODEL)
    (sdq, sdkb, sdvb, sfq, sfkb, sfvb, sdk, sdv, sfk, sfv, slf, _, sga, sgb, _, _) = _inproj(
        xs, g_attn[l], w_in[l], b_f[l], ns)
    rel_page = past - (past - page + jnp.arange(page))
    bias_pg = jnp.repeat(_shifted_bias(rel_page, rel_table), 2, axis=0)
    bias_self = jnp.repeat(_shifted_bias(jnp.zeros((1,), I32), rel_table), 2, axis=0)
    feat_major = lambda c: jnp.moveaxis(c.reshape(npool, page, 512), 1, 2)
    lfp = _page_prefix(jnp.swapaxes(cache_fox_logf[l], 1, 2).reshape(npool * FOX_HEADS, page), 1024)
    sod, sof = _sample_attn(
        page_table, sdq, sdkb, sdvb, sfq, sfkb, sfvb, slf,
        feat_major(cache_diff_k[l]), cache_diff_v[l].reshape(npool, page * DIFF_HEADS, DIFF_V_DIM),
        feat_major(cache_fox_k[l]), feat_major(cache_fox_v[l]),
        lfp.reshape(npool, FOX_HEADS, page), bias_pg, bias_self, *lams, g_subln[l], lam_init)
    ys = _channel(xs, sod, sof, sga, sgb, p_sample[l].reshape(ns, PLE_DIM), lw, u_tab, v_tab,
                  g_final, ns, ns, 32)
    outs_s = (sdk.reshape(1, ns, 1, DIFF_HEADS, 2, DIFF_HEAD_DIM),
              sdv.reshape(1, ns, 1, DIFF_HEADS, DIFF_V_DIM),
              sfk.reshape(1, ns, 1, FOX_HEADS, FOX_HEAD_DIM),
              sfv.reshape(1, ns, 1, FOX_HEADS, FOX_HEAD_DIM),
              slf.reshape(1, ns, 1, FOX_HEADS))
    return (yp.reshape(batch, seq, D_MODEL), ys.reshape(ns, 1, D_MODEL)) + outs_p + outs_s
```
